```python
import jax, jax.numpy as jnp
from jax import lax
import numpy as np

D_MODEL = 2048
BATCH = 2
SEQ = 4096
DEPTH = 2
DEC_BATCH = 128
DEC_SEQ = 8
PAST_LEN = 2048
PAGE_SIZE = 128

N_HEADS = 16
N_KV = 4
GROUP = N_HEADS // N_KV
HEAD_DIM = D_MODEL // 32
NSA_WIDTH = N_HEADS * HEAD_DIM
BLK = 64
CMP_HIDDEN = 2 * HEAD_DIM
TOP_N = 16
WINDOW = 512
Q_BLOCK = 128
POOL_WINDOWS = (2, 4, 8, 16)
POOL_GROUPS = 4
POOL_WIDTH = D_MODEL // 2
POOL_GW = POOL_WIDTH // POOL_GROUPS
POOL_STATE = max(POOL_WINDOWS) - 1
D_FF = 4 * D_MODEL
EPS = 1e-6
NEG = -1e30
FORCED = 1e4

KV_COLS = 2 * N_KV * HEAD_DIM
SPLIT_SIZES = (NSA_WIDTH, KV_COLS, KV_COLS, KV_COLS, 3 * N_HEADS, POOL_WIDTH, 2 * D_MODEL)
IN_COLS = sum(SPLIT_SIZES)
SPLITS = tuple(int(v) for v in np.cumsum(SPLIT_SIZES)[:-1])

kernel_name = 'nsa_pool_gated_hybrid_step'

F32 = jnp.float32


def rmsnorm(x, g):
    xf = x.astype(F32)
    y = xf * lax.rsqrt(jnp.mean(xf * xf, axis=-1, keepdims=True) + EPS)
    return (y * g.astype(F32)).astype(x.dtype)


def project_in(x, g, w):
    n, T, _ = x.shape
    z = rmsnorm(x, g) @ w
    q, kc, ks, kw, ng, u, mg = jnp.split(z, SPLITS, axis=-1)
    q = q.reshape(n, T, N_KV, GROUP, HEAD_DIM)
    kv = lambda a: a.reshape(n, T, 2, N_KV, HEAD_DIM)
    ng = jnp.moveaxis(jax.nn.sigmoid(ng.astype(F32)).reshape(n, T, 3, N_KV, GROUP), 2, -1).astype(x.dtype)
    mg = jax.nn.sigmoid(mg.astype(F32)).reshape(n, T, 2, D_MODEL).astype(x.dtype)
    return q, kv(kc), kv(ks), kv(kw), ng, u, mg


def compress(rows, pe, w1, w2):
    n, L = rows.shape[:2]
    nb = L // BLK
    blk = rows.reshape(n, nb, BLK, N_KV, HEAD_DIM) + pe[None, None, :, None, :]
    flat = blk.transpose(0, 1, 3, 2, 4).reshape(n, nb, N_KV, BLK * HEAD_DIM)
    return jax.nn.gelu(flat @ w1) @ w2


def nsa_core(q, q_pos, k_cmp, v_cmp, k_sel, v_sel, k_win, v_win, win_pos, gate):
    scale = HEAD_DIM ** -0.5
    nb = k_cmp.shape[1]
    bidx = jnp.arange(nb)
    qp = q_pos[:, None]
    cmask = (bidx * BLK + BLK - 1)[None, :] <= qp
    s = jnp.einsum('nqhgd,nbhd->nhgqb', q, k_cmp, preferred_element_type=F32) * scale
    p = jnp.where(cmask, jax.nn.softmax(jnp.where(cmask, s, NEG), axis=-1), 0.0)
    o_cmp = jnp.einsum('nhgqb,nbhd->nqhgd', p.astype(v_cmp.dtype), v_cmp)
    cur = qp // BLK
    forced = (bidx == 0) | (bidx == cur) | (bidx == cur - 1)
    cand = bidx * BLK <= qp
    score = jnp.where(cand, jnp.where(forced, FORCED, p.sum(axis=2)), NEG)
    top_s, top_i = lax.top_k(score, min(TOP_N, nb))
    gather = jax.vmap(jax.vmap(lambda a, i: a[i]))
    gk = gather(k_sel, top_i)
    gv = gather(v_sel, top_i)
    kpos = top_i[..., None] * BLK + jnp.arange(BLK)
    smask = (top_s > NEG / 2)[..., None] & (kpos <= q_pos[:, None, None])
    s2 = jnp.einsum('nqhgd,nhqkbd->nhgqkb', q, gk, preferred_element_type=F32) * scale
    s2 = jnp.where(smask[:, :, None], s2, NEG)
    sh = s2.shape
    p2 = jax.nn.softmax(s2.reshape(sh[:4] + (-1,)), axis=-1).reshape(sh)
    o_sel = jnp.einsum('nhgqkb,nhqkbd->nqhgd', p2.astype(gv.dtype), gv)
    wp = win_pos[None, :]
    wmask = (wp <= qp) & (wp >= qp - WINDOW) & (wp >= 0)
    s3 = jnp.einsum('nqhgd,nshd->nhgqs', q, k_win, preferred_element_type=F32) * scale
    p3 = jax.nn.softmax(jnp.where(wmask, s3, NEG), axis=-1)
    o_win = jnp.einsum('nhgqs,nshd->nqhgd', p3.astype(v_win.dtype), v_win)
    g = gate.astype(o_cmp.dtype)
    o = g[..., 0:1] * o_cmp + g[..., 1:2] * o_sel + g[..., 2:3] * o_win
    return o.reshape(o.shape[:2] + (NSA_WIDTH,))


def nsa_prompt(q, kvc, kvs, kvw, gate, cw, cv):
    n, T = q.shape[:2]
    nb = T // BLK
    k_cmp = compress(kvc[:, :, 0], *cw)
    v_cmp = compress(kvc[:, :, 1], *cv)
    sel = kvs.reshape(n, nb, BLK, 2, N_KV, HEAD_DIM).transpose(3, 0, 4, 1, 2, 5)
    nqb = T // Q_BLOCK
    kw = jnp.pad(kvw, ((0, 0), (WINDOW, 0), (0, 0), (0, 0), (0, 0)))
    widx = jnp.arange(nqb)[:, None] * Q_BLOCK + jnp.arange(WINDOW + Q_BLOCK)[None, :]
    wblk = jnp.moveaxis(kw[:, widx], 1, 0)
    wpos = widx - WINDOW
    qb = jnp.moveaxis(q.reshape(n, nqb, Q_BLOCK, N_KV, GROUP, HEAD_DIM), 1, 0)
    gb = jnp.moveaxis(gate.reshape(n, nqb, Q_BLOCK, N_KV, GROUP, 3), 1, 0)
    pos = jnp.arange(T).reshape(nqb, Q_BLOCK)

    def step(args):
        qi, gi, wi, wpi, pi = args
        return nsa_core(qi, pi, k_cmp, v_cmp, sel[0], sel[1], wi[:, :, 0], wi[:, :, 1], wpi, gi)

    out = lax.map(step, (qb, gb, wblk, wpos, pos))
    return jnp.moveaxis(out, 0, 1).reshape(n, T, NSA_WIDTH)


def nsa_sample(q, kvc, kvs, kvw, gate, cache_cmp, cache_sel, cache_win, page_table, cw, cv):
    n, T = q.shape[:2]
    L = PAST_LEN + T
    Lp = -(-L // BLK) * BLK

    def full_rows(cache, new):
        past = cache[page_table].reshape(n, PAST_LEN, 2, N_KV, HEAD_DIM)
        rows = jnp.concatenate([past, new], axis=1)
        return jnp.pad(rows, ((0, 0), (0, Lp - L), (0, 0), (0, 0), (0, 0)))

    rc = full_rows(cache_cmp, kvc)
    k_cmp = compress(rc[:, :, 0], *cw)
    v_cmp = compress(rc[:, :, 1], *cv)
    rs = full_rows(cache_sel, kvs)
    sel = rs.reshape(n, Lp // BLK, BLK, 2, N_KV, HEAD_DIM).transpose(3, 0, 4, 1, 2, 5)
    wrows = jnp.concatenate([cache_win, kvw], axis=1)
    wb = cache_win.shape[1]
    wpos = PAST_LEN - wb + jnp.arange(wb + T)
    pos = PAST_LEN + jnp.arange(T)
    qs = jnp.moveaxis(q, 1, 0)[:, :, None]
    gs = jnp.moveaxis(gate, 1, 0)[:, :, None]

    def step(args):
        qi, gi, pi = args
        return nsa_core(qi, pi[None], k_cmp, v_cmp, sel[0], sel[1], wrows[:, :, 0], wrows[:, :, 1], wpos, gi)

    out = lax.map(step, (qs, gs, pos))
    new_win = wrows[:, -min(WINDOW, wb + T):]
    return jnp.moveaxis(out[:, :, 0], 0, 1), new_win


def pool_mix(u, past, w_grp, scale):
    n, T, _ = u.shape
    xx = u if past is None else jnp.concatenate([past, u], axis=1)
    P = xx.shape[1] - T
    cs = jnp.pad(jnp.cumsum(xx.astype(F32), axis=1), ((0, 0), (1, 0), (0, 0)))
    end = P + jnp.arange(T) + 1
    uf = u.astype(F32)
    outs = []
    for gi, w in enumerate(POOL_WINDOWS):
        c0, c1 = gi * POOL_GW, (gi + 1) * POOL_GW
        start = jnp.maximum(end - w, 0)
        cnt = (end - start).astype(F32)[None, :, None]
        cg = cs[:, :, c0:c1]
        outs.append((cg[:, end] - cg[:, start]) / cnt - uf[..., c0:c1])
    d = jnp.stack(outs, axis=2).astype(u.dtype)
    y = jnp.einsum('ntgc,gce->ntge', d, w_grp).reshape(n, T, POOL_WIDTH) * scale
    return y, xx[:, -POOL_STATE:]


def merge_and_mlp(x, o_nsa, o_pool, mg, w_br_nsa, w_br_pool, w_o, g_mlp, w_up, w_down):
    m = mg[:, :, 0] * (o_nsa @ w_br_nsa) + mg[:, :, 1] * (o_pool @ w_br_pool)
    x = x + m @ w_o
    h = rmsnorm(x, g_mlp)
    return x + jnp.square(jax.nn.relu(h @ w_up)) @ w_down


def setup_inputs(seed: int = 0) -> dict:
    key = jax.random.key(seed)
    ks = jax.random.split(key, 32)
    nrm = lambda k, shape, fan: jax.random.normal(k, shape, F32) * fan ** -0.5
    rnd = lambda k, shape: jax.random.normal(k, shape, F32)
    n_pages = PAST_LEN // PAGE_SIZE
    n_used = DEC_BATCH * n_pages
    n_phys = n_used + max(1, n_used // 4)
    wb = min(WINDOW, PAST_LEN)
    page_table = jax.random.permutation(ks[6], n_phys)[:n_used].reshape(DEC_BATCH, n_pages).astype(jnp.int32)
    return {
        'x_prompt': rnd(ks[0], (BATCH, SEQ, D_MODEL)),
        'x_sample': rnd(ks[1], (DEC_BATCH, DEC_SEQ, D_MODEL)),
        'cache_cmp_kv': rnd(ks[2], (DEPTH, n_phys, PAGE_SIZE, 2, N_KV, HEAD_DIM)),
        'cache_sel_kv': rnd(ks[3], (DEPTH, n_phys, PAGE_SIZE, 2, N_KV, HEAD_DIM)),
        'cache_win_kv': rnd(ks[4], (DEPTH, DEC_BATCH, wb, 2, N_KV, HEAD_DIM)),
        'state_pool': rnd(ks[5], (DEPTH, DEC_BATCH, POOL_STATE, POOL_WIDTH)),
        'page_table': page_table,
        'g_mix': 1.0 + 0.1 * rnd(ks[7], (DEPTH, D_MODEL)),
        'w_in': nrm(ks[8], (DEPTH, D_MODEL, IN_COLS), D_MODEL),
        'pe_ck': 0.1 * rnd(ks[9], (DEPTH, BLK, HEAD_DIM)),
        'w_ck1': nrm(ks[10], (DEPTH, BLK * HEAD_DIM, CMP_HIDDEN), BLK * HEAD_DIM),
        'w_ck2': nrm(ks[11], (DEPTH, CMP_HIDDEN, HEAD_DIM), CMP_HIDDEN),
        'pe_cv': 0.1 * rnd(ks[12], (DEPTH, BLK, HEAD_DIM)),
        'w_cv1': nrm(ks[13], (DEPTH, BLK * HEAD_DIM, CMP_HIDDEN), BLK * HEAD_DIM),
        'w_cv2': nrm(ks[14], (DEPTH, CMP_HIDDEN, HEAD_DIM), CMP_HIDDEN),
        'w_pool_grp': nrm(ks[15], (DEPTH, POOL_GROUPS, POOL_GW, POOL_GW), POOL_GW),
        'pool_scale': 1.0 + 0.1 * rnd(ks[16], (DEPTH, POOL_WIDTH)),
        'w_br_nsa': nrm(ks[17], (DEPTH, NSA_WIDTH, D_MODEL), NSA_WIDTH),
        'w_br_pool': nrm(ks[18], (DEPTH, POOL_WIDTH, D_MODEL), POOL_WIDTH),
        'w_o': nrm(ks[19], (DEPTH, D_MODEL, D_MODEL), D_MODEL),
        'g_mlp': 1.0 + 0.1 * rnd(ks[20], (DEPTH, D_MODEL)),
        'w_up': nrm(ks[21], (DEPTH, D_MODEL, D_FF), D_MODEL),
        'w_down': nrm(ks[22], (DEPTH, D_FF, D_MODEL), D_FF),
        'g_final': 1.0 + 0.1 * rnd(ks[23], (D_MODEL,)),
    }


def reference(x_prompt, x_sample, cache_cmp_kv, cache_sel_kv, cache_win_kv, state_pool, page_table,
              g_mix, w_in, pe_ck, w_ck1, w_ck2, pe_cv, w_cv1, w_cv2, w_pool_grp, pool_scale,
              w_br_nsa, w_br_pool, w_o, g_mlp, w_up, w_down, g_final):
    xp, xs = x_prompt, x_sample
    p_cmp, p_sel, p_win, p_pool = [], [], [], []
    s_cmp, s_sel, s_win, s_pool = [], [], [], []
    for l in range(DEPTH):
        cw = (pe_ck[l], w_ck1[l], w_ck2[l])
        cv = (pe_cv[l], w_cv1[l], w_cv2[l])
        tail = (w_br_nsa[l], w_br_pool[l], w_o[l], g_mlp[l], w_up[l], w_down[l])
        q, kvc, kvs, kvw, ng, u, mg = project_in(xp, g_mix[l], w_in[l])
        o_nsa = nsa_prompt(q, kvc, kvs, kvw, ng, cw, cv)
        o_pool, pst = pool_mix(u, None, w_pool_grp[l], pool_scale[l])
        xp = merge_and_mlp(xp, o_nsa, o_pool, mg, *tail)
        p_cmp.append(kvc)
        p_sel.append(kvs)
        p_win.append(kvw[:, -min(WINDOW, kvw.shape[1]):])
        p_pool.append(pst)
        q, kvc, kvs, kvw, ng, u, mg = project_in(xs, g_mix[l], w_in[l])
        o_nsa, new_win = nsa_sample(q, kvc, kvs, kvw, ng, cache_cmp_kv[l], cache_sel_kv[l],
                                    cache_win_kv[l], page_table, cw, cv)
        o_pool, sst = pool_mix(u, state_pool[l], w_pool_grp[l], pool_scale[l])
        xs = merge_and_mlp(xs, o_nsa, o_pool, mg, *tail)
        s_cmp.append(kvc)
        s_sel.append(kvs)
        s_win.append(new_win)
        s_pool.append(sst)
    y_prompt = rmsnorm(xp, g_final)
    y_sample = rmsnorm(xs, g_final)
    return (y_prompt, y_sample, jnp.stack(p_cmp), jnp.stack(p_sel), jnp.stack(p_win), jnp.stack(p_pool),
            jnp.stack(s_cmp), jnp.stack(s_sel), jnp.stack(s_win), jnp.stack(s_pool))
```

```python
import functools

import jax
import jax.numpy as jnp
from jax import lax
from jax.experimental import pallas as pl
from jax.experimental.pallas import tpu as pltpu

F32 = jnp.float32
BF16 = jnp.bfloat16

N_KV = 4
GROUP = 4
HEAD_DIM = 64
KV_W = N_KV * HEAD_DIM
BLK = 64
TOP_N = 16
WINDOW = 512
Q_TILE = 128
KEY_TILE = 512
POOL_WINDOWS = (2, 4, 8, 16)
POOL_HALO = 16
EPS = 1e-6
NEG = -1e30
FORCED = 1e4
LANES = 128
VMEM_LIMIT = 56 * 1024 * 1024

_NT = (((1,), (1,)), ((), ()))


def _params(*sem):
    return pltpu.CompilerParams(dimension_semantics=sem, vmem_limit_bytes=VMEM_LIMIT)


def _norm_matmul_kernel(x_ref, g_ref, w_ref, o_ref, xn_ref, *, sigmoid):
    @pl.when(pl.program_id(1) == 0)
    def _():
        x = x_ref[...]
        ms = jnp.mean(x * x, axis=-1, keepdims=True)
        xn_ref[...] = (x * lax.rsqrt(ms + EPS) * g_ref[...]).astype(BF16)

    z = jnp.dot(xn_ref[...], w_ref[...], preferred_element_type=F32)
    if sigmoid:
        z = jax.nn.sigmoid(z)
    o_ref[...] = z.astype(o_ref.dtype)


def _norm_matmul(x, g, w, out_dtype, sigmoid=False, tm=1024, tn=512):
    m, d = x.shape
    n = w.shape[1]
    tm, tn = min(tm, m), min(tn, n)
    return pl.pallas_call(
        functools.partial(_norm_matmul_kernel, sigmoid=sigmoid),
        grid=(m // tm, n // tn),
        in_specs=[
            pl.BlockSpec((tm, d), lambda i, j: (i, 0)),
            pl.BlockSpec((1, d), lambda i, j: (0, 0)),
            pl.BlockSpec((d, tn), lambda i, j: (0, j)),
        ],
        out_specs=pl.BlockSpec((tm, tn), lambda i, j: (i, j)),
        out_shape=jax.ShapeDtypeStruct((m, n), out_dtype),
        scratch_shapes=[pltpu.VMEM((tm, d), BF16)],
        compiler_params=_params("parallel", "arbitrary"),
    )(x, g.reshape(1, d), w)


def _compress_kernel(x_ref, pe_ref, w1_ref, w2_ref, o_ref, *, tb):
    acc = jnp.zeros((tb, 2 * LANES), F32)
    for r in range(BLK):
        xr = x_ref[pl.ds(r, tb, stride=BLK), :] + pe_ref[0, r:r + 1, :]
        acc = acc + jnp.dot(xr.astype(BF16), w1_ref[0, r], preferred_element_type=F32)
    hid = jax.nn.gelu(acc)
    o_ref[...] = jnp.dot(hid.astype(BF16), w2_ref[0], preferred_element_type=F32)


def _compress(rows, n_blocks, pe2, w1bd, w2bd, tb=128):
    return pl.pallas_call(
        functools.partial(_compress_kernel, tb=tb),
        grid=(n_blocks // tb, 4),
        in_specs=[
            pl.BlockSpec((tb * BLK, LANES), lambda i, c: (i, c)),
            pl.BlockSpec((1, BLK, LANES), lambda i, c: (c // 2, 0, 0)),
            pl.BlockSpec((1, BLK, LANES, 2 * LANES), lambda i, c: (c // 2, 0, 0, 0)),
            pl.BlockSpec((1, 2 * LANES, LANES), lambda i, c: (c // 2, 0, 0)),
        ],
        out_specs=pl.BlockSpec((tb, LANES), lambda i, c: (i, c)),
        out_shape=jax.ShapeDtypeStruct((n_blocks, 4 * LANES), F32),
        compiler_params=_params("parallel", "arbitrary"),
    )(rows, pe2, w1bd, w2bd)


def _compress_weights(pe_k, w1_k, w2_k, pe_v, w1_v, w2_v):
    def one(pe, w1, w2):
        hid = w1.shape[1]
        w1r = w1.reshape(BLK, HEAD_DIM, hid)
        z = jnp.zeros_like(w1r)
        w1bd = jnp.concatenate(
            [jnp.concatenate([w1r, z], axis=2), jnp.concatenate([z, w1r], axis=2)], axis=1)
        z2 = jnp.zeros_like(w2)
        w2bd = jnp.concatenate(
            [jnp.concatenate([w2, z2], axis=1), jnp.concatenate([z2, w2], axis=1)], axis=0)
        return jnp.concatenate([pe, pe], axis=1), w1bd.astype(BF16), w2bd.astype(BF16)

    k, v = one(pe_k, w1_k, w2_k), one(pe_v, w1_v, w2_v)
    return tuple(jnp.stack([a, b]) for a, b in zip(k, v))


def _pool_kernel(u_ref, halo_ref, w_ref, s_ref, o_ref, *, tt, tiles_per_seq):
    i = pl.program_id(0)
    u = u_ref[...]
    halo = halo_ref[...]
    if tiles_per_seq is not None:
        halo = jnp.where(i % tiles_per_seq == 0, 0.0, halo)
        pos = (i % tiles_per_seq) * tt + lax.broadcasted_iota(jnp.int32, (tt, 1), 0)
    a = jnp.concatenate([halo, u], axis=0)
    gw = u.shape[1] // len(POOL_WINDOWS)
    for gi, w in enumerate(POOL_WINDOWS):
        b = a[:, gi * gw:(gi + 1) * gw]
        width = 1
        while width < w:
            b = b[:b.shape[0] - width] + b[width:]
            width *= 2
        lo = POOL_HALO + 1 - w
        win = b[lo:lo + tt]
        if tiles_per_seq is None:
            mean = win / float(w)
        else:
            mean = win / jnp.minimum(pos + 1, w).astype(F32)
        d = mean - u[:, gi * gw:(gi + 1) * gw]
        y = jnp.dot(d.astype(BF16), w_ref[gi], preferred_element_type=F32)
        o_ref[:, gi * gw:(gi + 1) * gw] = (y * s_ref[:, gi * gw:(gi + 1) * gw]).astype(o_ref.dtype)


def _pool_mix(u, w_grp, scale, tt, tiles_per_seq):
    m, width = u.shape
    hb = tt // POOL_HALO
    return pl.pallas_call(
        functools.partial(_pool_kernel, tt=tt, tiles_per_seq=tiles_per_seq),
        grid=(m // tt,),
        in_specs=[
            pl.BlockSpec((tt, width), lambda i: (i, 0)),
            pl.BlockSpec((POOL_HALO, width), lambda i: (jnp.maximum(i * hb - 1, 0), 0)),
            pl.BlockSpec(w_grp.shape, lambda i: (0, 0, 0)),
            pl.BlockSpec((1, width), lambda i: (0, 0)),
        ],
        out_specs=pl.BlockSpec((tt, width), lambda i: (i, 0)),
        out_shape=jax.ShapeDtypeStruct((m, width), BF16),
        compiler_params=_params("parallel"),
    )(u, u, w_grp, scale.reshape(1, width))


def _gated_branch_kernel(a_ref, b_ref, wa_ref, wb_ref, ga_ref, gb_ref, o_ref):
    a = jnp.dot(a_ref[...], wa_ref[...], preferred_element_type=F32)
    b = jnp.dot(b_ref[...], wb_ref[...], preferred_element_type=F32)
    o_ref[...] = (ga_ref[...].astype(F32) * a + gb_ref[...].astype(F32) * b).astype(o_ref.dtype)


def _gated_branches(a, b, wa, wb, gates, tm=1024, tn=512):
    m, k = a.shape
    n = wa.shape[1]
    nj = n // tn
    return pl.pallas_call(
        _gated_branch_kernel,
        grid=(m // tm, nj),
        in_specs=[
            pl.BlockSpec((tm, k), lambda i, j: (i, 0)),
            pl.BlockSpec((tm, k), lambda i, j: (i, 0)),
            pl.BlockSpec((k, tn), lambda i, j: (0, j)),
            pl.BlockSpec((k, tn), lambda i, j: (0, j)),
            pl.BlockSpec((tm, tn), lambda i, j: (i, j)),
            pl.BlockSpec((tm, tn), lambda i, j: (i, j + nj)),
        ],
        out_specs=pl.BlockSpec((tm, tn), lambda i, j: (i, j)),
        out_shape=jax.ShapeDtypeStruct((m, n), BF16),
        compiler_params=_params("parallel", "arbitrary"),
    )(a, b, wa, wb, gates, gates)


def _resid_matmul_kernel(x_ref, a_ref, w_ref, o_ref):
    o_ref[...] = x_ref[...] + jnp.dot(a_ref[...], w_ref[...], preferred_element_type=F32)


def _resid_matmul(x, a, w, tm=1024, tn=512):
    m, k = a.shape
    n = w.shape[1]
    return pl.pallas_call(
        _resid_matmul_kernel,
        grid=(m // tm, n // tn),
        in_specs=[
            pl.BlockSpec((tm, tn), lambda i, j: (i, j)),
            pl.BlockSpec((tm, k), lambda i, j: (i, 0)),
            pl.BlockSpec((k, tn), lambda i, j: (0, j)),
        ],
        out_specs=pl.BlockSpec((tm, tn), lambda i, j: (i, j)),
        out_shape=jax.ShapeDtypeStruct((m, n), F32),
        compiler_params=_params("parallel", "arbitrary"),
    )(x, a, w)


def _mlp_kernel(x_ref, g_ref, wu_ref, wd_ref, gf_ref, o_ref, h_ref, acc_ref, *, final_norm):
    f = pl.program_id(1)

    @pl.when(f == 0)
    def _():
        x = x_ref[...]
        ms = jnp.mean(x * x, axis=-1, keepdims=True)
        h_ref[...] = (x * lax.rsqrt(ms + EPS) * g_ref[...]).astype(BF16)
        acc_ref[...] = jnp.zeros_like(acc_ref)

    up = jnp.dot(h_ref[...], wu_ref[...], preferred_element_type=F32)
    act = jnp.square(jnp.maximum(up, 0.0)).astype(BF16)
    acc_ref[...] += jnp.dot(act, wd_ref[...], preferred_element_type=F32)

    @pl.when(f == pl.num_programs(1) - 1)
    def _():
        y = x_ref[...] + acc_ref[...]
        if final_norm:
            ms = jnp.mean(y * y, axis=-1, keepdims=True)
            y = y * lax.rsqrt(ms + EPS) * gf_ref[...]
        o_ref[...] = y


def _mlp(x, g, w_up, w_down, g_final, final_norm, tm=512, tf=512):
    m, d = x.shape
    dff = w_up.shape[1]
    return pl.pallas_call(
        functools.partial(_mlp_kernel, final_norm=final_norm),
        grid=(m // tm, dff // tf),
        in_specs=[
            pl.BlockSpec((tm, d), lambda i, f: (i, 0)),
            pl.BlockSpec((1, d), lambda i, f: (0, 0)),
            pl.BlockSpec((d, tf), lambda i, f: (0, f)),
            pl.BlockSpec((tf, d), lambda i, f: (f, 0)),
            pl.BlockSpec((1, d), lambda i, f: (0, 0)),
        ],
        out_specs=pl.BlockSpec((tm, d), lambda i, f: (i, 0)),
        out_shape=jax.ShapeDtypeStruct((m, d), F32),
        scratch_shapes=[pltpu.VMEM((tm, d), BF16), pltpu.VMEM((tm, d), F32)],
        compiler_params=_params("parallel", "arbitrary"),
    )(x, g.reshape(1, d), w_up, w_down, g_final.reshape(1, d))


def _shr(x, pow2):
    assert pow2 & (pow2 - 1) == 0
    return lax.shift_right_logical(x, pow2.bit_length() - 1)


def _masked_softmax(s, mask):
    sm = jnp.where(mask, s, NEG)
    e = jnp.exp(sm - jnp.max(sm, axis=-1, keepdims=True))
    return e / jnp.sum(e, axis=-1, keepdims=True)


def _select_blocks(imp, qpos, n_loop):
    r, nb = imp.shape
    bidx = lax.broadcasted_iota(jnp.int32, (r, nb), 1)
    cur = _shr(qpos, BLK)
    forced = (bidx == 0) | (bidx == cur) | (bidx == cur - 1)
    cand = bidx * BLK <= qpos
    score = jnp.where(cand, jnp.where(forced, FORCED, imp), NEG)
    rank = jnp.zeros((r, nb), F32)
    for j in range(n_loop):
        c = score[:, j:j + 1]
        beats = (c > score) | ((c == score) & (bidx > j))
        rank = rank + jnp.where(beats, 1.0, 0.0)
    return jnp.where(cand & (rank < TOP_N), 1.0, 0.0)


def _block_expand(n_blocks, k0, n_keys):
    b = lax.broadcasted_iota(jnp.int32, (n_blocks, n_keys), 0)
    k = lax.broadcasted_iota(jnp.int32, (n_blocks, n_keys), 1) + k0
    return jnp.where(_shr(k, BLK) == b, 1.0, 0.0).astype(BF16)


def _nsa_prompt_kernel(q_ref, kc_ref, vc_ref, ks_ref, vs_ref, kw_ref, vw_ref, g_ref,
                       o_ref, mask_ref, *, seq):
    i = pl.program_id(2)
    rows = GROUP * Q_TILE
    qt = q_ref[...]
    q = jnp.concatenate([qt[:, g * HEAD_DIM:(g + 1) * HEAD_DIM] for g in range(GROUP)], axis=0)
    tok = lax.broadcasted_iota(jnp.int32, (Q_TILE, 1), 0)
    qpos1 = i * Q_TILE + tok
    qpos = jnp.concatenate([qpos1] * GROUP, axis=0)
    nb = seq // BLK

    s = lax.dot_general(q, kc_ref[0, 0], _NT, preferred_element_type=F32)
    bidx = lax.broadcasted_iota(jnp.int32, (1, nb), 1)
    cmask = bidx * BLK + (BLK - 1) <= qpos
    p = jnp.where(cmask, _masked_softmax(s, cmask), 0.0)
    o_cmp = jnp.dot(p.astype(BF16), vc_ref[0, 0], preferred_element_type=F32)

    imp = p[0:Q_TILE]
    for g in range(1, GROUP):
        imp = imp + p[g * Q_TILE:(g + 1) * Q_TILE]
    sel = _select_blocks(imp, qpos1, nb).astype(BF16)
    for j in range(seq // KEY_TILE):
        mask_ref[j] = jnp.dot(sel, _block_expand(nb, j * KEY_TILE, KEY_TILE),
                              preferred_element_type=F32)

    def body(j, carry):
        m, l, acc = carry
        k0 = pl.multiple_of(j * KEY_TILE, KEY_TILE)
        k = ks_ref[0, 0, pl.ds(k0, KEY_TILE), :]
        v = vs_ref[0, 0, pl.ds(k0, KEY_TILE), :]
        sj = lax.dot_general(q, k, _NT, preferred_element_type=F32)
        kpos = k0 + lax.broadcasted_iota(jnp.int32, (1, KEY_TILE), 1)
        valid1 = jnp.where(kpos <= qpos1, mask_ref[j], 0.0)
        valid = jnp.concatenate([valid1] * GROUP, axis=0) > 0.5
        sj = jnp.where(valid, sj, NEG)
        m_new = jnp.maximum(m, jnp.max(sj, axis=-1, keepdims=True))
        alpha = jnp.exp(m - m_new)
        pj = jnp.exp(sj - m_new)
        l = alpha * l + jnp.sum(pj, axis=-1, keepdims=True)
        acc = alpha * acc + jnp.dot(pj.astype(BF16), v, preferred_element_type=F32)
        return m_new, l, acc

    n_tiles = lax.div(i * Q_TILE + Q_TILE - 1, KEY_TILE) + 1
    init = (jnp.full((rows, 1), NEG, F32), jnp.zeros((rows, 1), F32),
            jnp.zeros((rows, HEAD_DIM), F32))
    _, l, acc = lax.fori_loop(0, n_tiles, body, init)
    o_sel = acc / l

    span = WINDOW + Q_TILE
    start = pl.multiple_of(jnp.maximum(i * Q_TILE - WINDOW, 0), Q_TILE)
    kw = kw_ref[0, 0, pl.ds(start, span), :]
    vw = vw_ref[0, 0, pl.ds(start, span), :]
    sw = lax.dot_general(q, kw, _NT, preferred_element_type=F32)
    wpos = start + lax.broadcasted_iota(jnp.int32, (1, span), 1)
    wmask = (wpos <= qpos) & (wpos >= qpos - WINDOW)
    pw = _masked_softmax(sw, wmask)
    o_win = jnp.dot(pw.astype(BF16), vw, preferred_element_type=F32)

    gt = g_ref[...]
    outs = []
    for g in range(GROUP):
        sl = slice(g * Q_TILE, (g + 1) * Q_TILE)
        outs.append(gt[:, g:g + 1] * o_cmp[sl]
                    + gt[:, GROUP + g:GROUP + g + 1] * o_sel[sl]
                    + gt[:, 2 * GROUP + g:2 * GROUP + g + 1] * o_win[sl])
    o_ref[...] = jnp.concatenate(outs, axis=1).astype(o_ref.dtype)


def _nsa_prompt(q, kc, vc, ks, vs, kw, vw, gates, n, seq):
    nq = seq // Q_TILE
    nb = seq // BLK
    row = lambda b, h, i: (b * nq + i, h)
    head = lambda b, h, i: (b, h, 0, 0)
    kv_spec = pl.BlockSpec((1, 1, seq, HEAD_DIM), head)
    c_spec = pl.BlockSpec((1, 1, nb, HEAD_DIM), head)
    return pl.pallas_call(
        functools.partial(_nsa_prompt_kernel, seq=seq),
        grid=(n, N_KV, nq),
        in_specs=[
            pl.BlockSpec((Q_TILE, GROUP * HEAD_DIM), row),
            c_spec, c_spec, kv_spec, kv_spec, kv_spec, kv_spec,
            pl.BlockSpec((Q_TILE, LANES), row),
        ],
        out_specs=pl.BlockSpec((Q_TILE, GROUP * HEAD_DIM), row),
        out_shape=jax.ShapeDtypeStruct((n * seq, N_KV * GROUP * HEAD_DIM), BF16),
        scratch_shapes=[pltpu.VMEM((seq // KEY_TILE, Q_TILE, KEY_TILE), F32)],
        compiler_params=_params("parallel", "parallel", "arbitrary"),
    )(q, kc, vc, ks, vs, kw, vw, gates)


def _nsa_sample_kernel(pt_ref, q_ref, g_ref, *refs, n_pages, page, dec, wb):
    del pt_ref
    cmp_pages = refs[:n_pages]
    cmp_new = refs[n_pages]
    sel_pages = refs[n_pages + 1:2 * n_pages + 1]
    sel_new = refs[2 * n_pages + 1]
    win_ref, win_new, o_ref, kc_ref, k_ref, v_ref, kw_ref, vw_ref = refs[2 * n_pages + 2:]
    past = n_pages * page
    rows = N_KV * GROUP * dec
    q = q_ref[0]
    ridx = lax.broadcasted_iota(jnp.int32, (rows, 1), 0)
    qpos = past + (ridx & (dec - 1))

    bpp = page // BLK
    kc_ref[...] = jnp.zeros_like(kc_ref)
    for p_i in range(n_pages):
        kc_ref[bpp * p_i:bpp * (p_i + 1), :] = cmp_pages[p_i][0]
    kc_ref[bpp * n_pages:bpp * n_pages + 1, :] = cmp_new[0, 0:1, :]
    nbp = kc_ref.shape[0]
    s = lax.dot_general(q, kc_ref[:, :KV_W].astype(BF16), _NT, preferred_element_type=F32)
    bidx = lax.broadcasted_iota(jnp.int32, (1, nbp), 1)
    cmask = bidx * BLK + (BLK - 1) <= qpos
    p = jnp.where(cmask, _masked_softmax(s, cmask), 0.0)
    o_cmp = jnp.dot(p.astype(BF16), kc_ref[:, KV_W:].astype(BF16), preferred_element_type=F32)

    imps = []
    for h in range(N_KV):
        base = h * GROUP * dec
        acc = p[base:base + dec]
        for g in range(1, GROUP):
            acc = acc + p[base + g * dec:base + (g + 1) * dec]
        imps.append(acc)
    imp = jnp.concatenate(imps, axis=0)
    qpos_ht = past + (lax.broadcasted_iota(jnp.int32, (N_KV * dec, 1), 0) & (dec - 1))
    sel = _select_blocks(imp, qpos_ht, bpp * n_pages + 1)
    sel_rows = jnp.concatenate(
        [sel[h * dec:(h + 1) * dec] for h in range(N_KV) for _ in range(GROUP)], axis=0)

    n_keys = past + page
    for p_i in range(n_pages + 1):
        pg = sel_pages[p_i][0] if p_i < n_pages else sel_new[0]
        k_ref[p_i * page:(p_i + 1) * page, :] = pg[:, :KV_W].astype(BF16)
        v_ref[p_i * page:(p_i + 1) * page, :] = pg[:, KV_W:].astype(BF16)
    ss = lax.dot_general(q, k_ref[...], _NT, preferred_element_type=F32)
    key_sel = jnp.dot(sel_rows.astype(BF16), _block_expand(nbp, 0, n_keys),
                      preferred_element_type=F32)
    kpos = lax.broadcasted_iota(jnp.int32, (1, n_keys), 1)
    ps = _masked_softmax(ss, (key_sel > 0.5) & (kpos <= qpos))
    o_sel = jnp.dot(ps.astype(BF16), v_ref[...], preferred_element_type=F32)

    wn = win_ref[0]
    kw_ref[0:wb, :] = wn[:, :KV_W].astype(BF16)
    vw_ref[0:wb, :] = wn[:, KV_W:].astype(BF16)
    wn = win_new[0]
    kw_ref[wb:wb + page, :] = wn[:, :KV_W].astype(BF16)
    vw_ref[wb:wb + page, :] = wn[:, KV_W:].astype(BF16)
    sw = lax.dot_general(q, kw_ref[...], _NT, preferred_element_type=F32)
    wpos = past - wb + lax.broadcasted_iota(jnp.int32, (1, wb + page), 1)
    pw = _masked_softmax(sw, (wpos <= qpos) & (wpos >= qpos - WINDOW))
    o_win = jnp.dot(pw.astype(BF16), vw_ref[...], preferred_element_type=F32)

    gt = g_ref[0]
    o = gt[:, 0:1] * o_cmp + gt[:, 1:2] * o_sel + gt[:, 2:3] * o_win
    col = lax.broadcasted_iota(jnp.int32, (1, KV_W), 1)
    o = jnp.where(_shr(ridx, GROUP * dec) == _shr(col, HEAD_DIM), o, 0.0)
    hr = GROUP * dec
    out = o[0:hr]
    for h in range(1, N_KV):
        out = out + o[h * hr:(h + 1) * hr]
    o_ref[0] = out


def _nsa_sample(page_table, qbd, gates, cmp_phys, cmp_new, sel_cache, sel_new, win_cache, win_new):
    nseq, n_pages = page_table.shape
    page = sel_cache.shape[1]
    rows = qbd.shape[1]
    dec = rows // (N_KV * GROUP)
    wb = win_cache.shape[1]
    bpp = page // BLK
    nbp = LANES
    assert bpp * n_pages + 1 <= nbp

    def page_map(p_i):
        return lambda b, pt: (pt[b * n_pages + p_i], 0, 0)

    seq_map = lambda b, pt: (b, 0, 0)
    in_specs = [
        pl.BlockSpec((1, rows, KV_W), seq_map),
        pl.BlockSpec((1, rows, 8), seq_map),
    ]
    in_specs += [pl.BlockSpec((1, bpp, 2 * KV_W), page_map(p_i)) for p_i in range(n_pages)]
    in_specs += [pl.BlockSpec((1, 8, 2 * KV_W), seq_map)]
    in_specs += [pl.BlockSpec((1, page, 2 * KV_W), page_map(p_i)) for p_i in range(n_pages)]
    in_specs += [
        pl.BlockSpec((1, page, 2 * KV_W), seq_map),
        pl.BlockSpec((1, wb, 2 * KV_W), seq_map),
        pl.BlockSpec((1, page, 2 * KV_W), seq_map),
    ]
    n_keys = (n_pages + 1) * page
    grid_spec = pltpu.PrefetchScalarGridSpec(
        num_scalar_prefetch=1,
        grid=(nseq,),
        in_specs=in_specs,
        out_specs=pl.BlockSpec((1, GROUP * dec, KV_W), seq_map),
        scratch_shapes=[
            pltpu.VMEM((nbp, 2 * KV_W), F32),
            pltpu.VMEM((n_keys, KV_W), BF16),
            pltpu.VMEM((n_keys, KV_W), BF16),
            pltpu.VMEM((wb + page, KV_W), BF16),
            pltpu.VMEM((wb + page, KV_W), BF16),
        ],
    )
    return pl.pallas_call(
        functools.partial(_nsa_sample_kernel, n_pages=n_pages, page=page, dec=dec, wb=wb),
        grid_spec=grid_spec,
        out_shape=jax.ShapeDtypeStruct((nseq, GROUP * dec, KV_W), F32),
        compiler_params=_params("arbitrary"),
    )(page_table.reshape(-1), qbd, gates, *([cmp_phys] * n_pages), cmp_new,
      *([sel_cache] * n_pages), sel_new, win_cache, win_new)


def _heads_major(a, n, seq):
    return a.reshape(n, seq, N_KV, HEAD_DIM).transpose(0, 2, 1, 3).astype(BF16)


def _pad_rows(a, rows):
    return jnp.pad(a, ((0, 0), (0, rows - a.shape[1]), (0, 0)))


def kernel(x_prompt, x_sample, cache_cmp_kv, cache_sel_kv, cache_win_kv, state_pool, page_table, g_mix, w_in, pe_ck, w_ck1, w_ck2, pe_cv, w_cv1, w_cv2, w_pool_grp, pool_scale, w_br_nsa, w_br_pool, w_o, g_mlp, w_up, w_down, g_final):
    n, seq, d = x_prompt.shape
    nseq, dec, _ = x_sample.shape
    depth = w_in.shape[0]
    n_phys, page = cache_cmp_kv.shape[1], cache_cmp_kv.shape[2]
    wb = cache_win_kv.shape[2]
    mp, ms = n * seq, nseq * dec
    nsa_w = N_KV * GROUP * HEAD_DIM
    kv_cols = 2 * KV_W
    n_gate = 3 * N_KV * GROUP
    pool_w = state_pool.shape[-1]
    pool_state = state_pool.shape[2]
    c_q, c_kv, c_ng = nsa_w, nsa_w + 3 * kv_cols, nsa_w + 3 * kv_cols + n_gate
    c_u = c_ng + pool_w

    x = jnp.concatenate([x_prompt.reshape(mp, d), x_sample.reshape(ms, d)], axis=0)
    kv_shape = (2, N_KV, HEAD_DIM)
    outs = [[] for _ in range(8)]

    for l in range(depth):
        w = w_in[l]
        w_q = (w[:, :c_q] * HEAD_DIM ** -0.5).astype(BF16)
        w_kv = w[:, c_q:c_kv].astype(BF16)
        w_ng = w[:, c_kv:c_ng].reshape(d, 3, N_KV, GROUP).transpose(0, 2, 1, 3).reshape(d, N_KV, 3 * GROUP)
        w_ng = jnp.pad(w_ng, ((0, 0), (0, 0), (0, LANES - 3 * GROUP))).reshape(d, N_KV * LANES).astype(BF16)
        w_u = w[:, c_ng:c_u].astype(BF16)
        w_mg = w[:, c_u:].astype(BF16)

        zkv = _norm_matmul(x, g_mix[l], w_kv, F32)
        q = _norm_matmul(x, g_mix[l], w_q, BF16)
        u = _norm_matmul(x, g_mix[l], w_u, F32)
        ng = _norm_matmul(x, g_mix[l], w_ng, F32, sigmoid=True)
        mg = _norm_matmul(x, g_mix[l], w_mg, BF16, sigmoid=True)

        cw = _compress_weights(pe_ck[l], w_ck1[l], w_ck2[l], pe_cv[l], w_cv1[l], w_cv2[l])

        cmp_p = _compress(zkv, mp // BLK, *cw)
        kc_p = _heads_major(cmp_p[:, :KV_W], n, seq // BLK)
        vc_p = _heads_major(cmp_p[:, KV_W:], n, seq // BLK)
        zp = zkv[:mp]
        hm = lambda c: _heads_major(zp[:, c * KV_W:(c + 1) * KV_W], n, seq)
        o_nsa_p = _nsa_prompt(q, kc_p, vc_p, hm(2), hm(3), hm(4), hm(5), ng, n, seq)

        zs = zkv[mp:].reshape(nseq, dec, 3 * kv_cols)
        new_c = _pad_rows(zs[:, :, :kv_cols], page)
        new_s = _pad_rows(zs[:, :, kv_cols:2 * kv_cols], page)
        new_w = _pad_rows(zs[:, :, 2 * kv_cols:], page)
        cmp_phys = _compress(cache_cmp_kv[l].reshape(n_phys * page, kv_cols),
                             n_phys * page // BLK, *cw).reshape(n_phys, page // BLK, kv_cols)
        cmp_new = _compress(new_c.reshape(nseq * page, kv_cols), nseq * page // BLK, *cw)
        cmp_new = _pad_rows(cmp_new.reshape(nseq, page // BLK, kv_cols)[:, :1], 8)

        qs = q[mp:].reshape(nseq, dec, N_KV, GROUP, HEAD_DIM)
        eye = jnp.eye(N_KV, dtype=BF16)
        qbd = jnp.einsum('bthgd,hk->bhgtkd', qs, eye).reshape(nseq, N_KV * GROUP * dec, KV_W)
        gs = ng[mp:].reshape(nseq, dec, N_KV, LANES)[..., :3 * GROUP].reshape(nseq, dec, N_KV, 3, GROUP)
        gs = gs.transpose(0, 2, 4, 1, 3).reshape(nseq, N_KV * GROUP * dec, 3)
        gs = jnp.pad(gs, ((0, 0), (0, 0), (0, 5)))
        o_s = _nsa_sample(page_table, qbd, gs, cmp_phys, cmp_new,
                          cache_sel_kv[l].reshape(n_phys, page, kv_cols), new_s,
                          cache_win_kv[l].reshape(nseq, wb, kv_cols), new_w)
        o_nsa_s = o_s.reshape(nseq, GROUP, dec, N_KV, HEAD_DIM).transpose(0, 2, 3, 1, 4)
        o_nsa = jnp.concatenate([o_nsa_p, o_nsa_s.reshape(ms, nsa_w).astype(BF16)], axis=0)

        w_grp = w_pool_grp[l].astype(BF16)
        u_p, u_s = u[:mp], u[mp:].reshape(nseq, dec, pool_w)
        o_pool_p = _pool_mix(u_p, w_grp, pool_scale[l], 512, seq // 512)
        hist = jnp.pad(state_pool[l], ((0, 0), (POOL_HALO - pool_state, 0), (0, 0)))
        xx_s = jnp.concatenate([hist, u_s], axis=1)
        rows_s = POOL_HALO + dec
        o_pool_s = _pool_mix(xx_s.reshape(nseq * rows_s, pool_w), w_grp, pool_scale[l],
                             16 * rows_s, None)
        o_pool_s = o_pool_s.reshape(nseq, rows_s, pool_w)[:, POOL_HALO:].reshape(ms, pool_w)
        o_pool = jnp.concatenate([o_pool_p, o_pool_s], axis=0)

        m = _gated_branches(o_nsa, o_pool, w_br_nsa[l].astype(BF16), w_br_pool[l].astype(BF16), mg)
        x1 = _resid_matmul(x, m, w_o[l].astype(BF16))
        x = _mlp(x1, g_mlp[l], w_up[l].astype(BF16), w_down[l].astype(BF16), g_final,
                 final_norm=(l == depth - 1))

        zp6 = zp.reshape(n, seq, 3, *kv_shape)
        outs[0].append(zp6[:, :, 0])
        outs[1].append(zp6[:, :, 1])
        outs[2].append(zp6[:, -min(WINDOW, seq):, 2])
        outs[3].append(u_p.reshape(n, seq, pool_w)[:, -pool_state:])
        zs6 = zs.reshape(nseq, dec, 3, *kv_shape)
        outs[4].append(zs6[:, :, 0])
        outs[5].append(zs6[:, :, 1])
        outs[6].append(jnp.concatenate([cache_win_kv[l], zs6[:, :, 2]], axis=1)[:, -min(WINDOW, wb + dec):])
        outs[7].append(xx_s[:, -pool_state:])

    y_prompt = x[:mp].reshape(n, seq, d)
    y_sample = x[mp:].reshape(nseq, dec, d)
    return (y_prompt, y_sample) + tuple(jnp.stack(o) for o in outs)
```

```python
import functools

import jax
import jax.numpy as jnp
from jax import lax
from jax.experimental import pallas as pl
from jax.experimental.pallas import tpu as pltpu

F32 = jnp.float32
BF16 = jnp.bfloat16

N_KV = 4
GROUP = 4
HEAD_DIM = 64
KV_W = N_KV * HEAD_DIM
BLK = 64
TOP_N = 16
WINDOW = 512
Q_TILE = 128
KEY_TILE = 512
POOL_WINDOWS = (2, 4, 8, 16)
POOL_HALO = 16
EPS = 1e-6
NEG = -1e30
FORCED = 1e4
LANES = 128
SUBLANES = 8
VMEM_LIMIT = 56 * 1024 * 1024

_NT = (((1,), (1,)), ((), ()))


def _params(*sem):
    return pltpu.CompilerParams(dimension_semantics=sem, vmem_limit_bytes=VMEM_LIMIT)


def _norm_matmul_kernel(x_ref, g_ref, w_ref, o_ref, xn_ref, *, sigmoid):
    @pl.when(pl.program_id(1) == 0)
    def _():
        x = x_ref[...]
        ms = jnp.mean(x * x, axis=-1, keepdims=True)
        xn_ref[...] = (x * lax.rsqrt(ms + EPS) * g_ref[...]).astype(BF16)

    z = jnp.dot(xn_ref[...], w_ref[...], preferred_element_type=F32)
    if sigmoid:
        z = jax.nn.sigmoid(z)
    o_ref[...] = z.astype(o_ref.dtype)


def _norm_matmul(x, g, w, out_dtype, sigmoid=False, tm=1024, tn=512):
    m, d = x.shape
    n = w.shape[1]
    tm, tn = min(tm, m), min(tn, n)
    return pl.pallas_call(
        functools.partial(_norm_matmul_kernel, sigmoid=sigmoid),
        grid=(m // tm, n // tn),
        in_specs=[
            pl.BlockSpec((tm, d), lambda i, j: (i, 0)),
            pl.BlockSpec((1, d), lambda i, j: (0, 0)),
            pl.BlockSpec((d, tn), lambda i, j: (0, j)),
        ],
        out_specs=pl.BlockSpec((tm, tn), lambda i, j: (i, j)),
        out_shape=jax.ShapeDtypeStruct((m, n), out_dtype),
        scratch_shapes=[pltpu.VMEM((tm, d), BF16)],
        compiler_params=_params("parallel", "arbitrary"),
    )(x, g.reshape(1, d), w)


def _compress_kernel(x_ref, pe_ref, w1_ref, w2_ref, o_ref, *, tb):
    acc = jnp.zeros((tb, 2 * LANES), F32)
    for r in range(BLK):
        xr = x_ref[pl.ds(r, tb, stride=BLK), :] + pe_ref[0, r:r + 1, :]
        acc = acc + jnp.dot(xr.astype(BF16), w1_ref[0, r], preferred_element_type=F32)
    hid = jax.nn.gelu(acc)
    o_ref[...] = jnp.dot(hid.astype(BF16), w2_ref[0], preferred_element_type=F32)


def _compress(rows, n_blocks, pe2, w1bd, w2bd, tb=128):
    return pl.pallas_call(
        functools.partial(_compress_kernel, tb=tb),
        grid=(n_blocks // tb, 4),
        in_specs=[
            pl.BlockSpec((tb * BLK, LANES), lambda i, c: (i, c)),
            pl.BlockSpec((1, BLK, LANES), lambda i, c: (c // 2, 0, 0)),
            pl.BlockSpec((1, BLK, LANES, 2 * LANES), lambda i, c: (c // 2, 0, 0, 0)),
            pl.BlockSpec((1, 2 * LANES, LANES), lambda i, c: (c // 2, 0, 0)),
        ],
        out_specs=pl.BlockSpec((tb, LANES), lambda i, c: (i, c)),
        out_shape=jax.ShapeDtypeStruct((n_blocks, 4 * LANES), F32),
        compiler_params=_params("parallel", "arbitrary"),
    )(rows, pe2, w1bd, w2bd)


def _compress_weights(pe_k, w1_k, w2_k, pe_v, w1_v, w2_v):
    def one(pe, w1, w2):
        hid = w1.shape[1]
        w1r = w1.reshape(BLK, HEAD_DIM, hid)
        z = jnp.zeros_like(w1r)
        w1bd = jnp.concatenate(
            [jnp.concatenate([w1r, z], axis=2), jnp.concatenate([z, w1r], axis=2)], axis=1)
        z2 = jnp.zeros_like(w2)
        w2bd = jnp.concatenate(
            [jnp.concatenate([w2, z2], axis=1), jnp.concatenate([z2, w2], axis=1)], axis=0)
        return jnp.concatenate([pe, pe], axis=1), w1bd.astype(BF16), w2bd.astype(BF16)

    k, v = one(pe_k, w1_k, w2_k), one(pe_v, w1_v, w2_v)
    return tuple(jnp.stack([a, b]) for a, b in zip(k, v))


def _compress_pages_kernel(x_ref, pe_ref, w1_ref, w2_ref, o_ref, *, rows):
    ridx = lax.broadcasted_iota(jnp.int32, (rows, 1), 0)
    is_v = (_shr(ridx, N_KV) & 1) == 1
    acc = jnp.zeros((rows, 2 * LANES), F32)
    for d in range(HEAD_DIM):
        a = x_ref[pl.ds(d, rows, stride=HEAD_DIM), :]
        ak = jnp.where(is_v, 0.0, a + pe_ref[0, d:d + 1, :])
        av = jnp.where(is_v, a + pe_ref[1, d:d + 1, :], 0.0)
        lhs = jnp.concatenate([ak, av], axis=1).astype(BF16)
        acc = acc + jnp.dot(lhs, w1_ref[d], preferred_element_type=F32)
    hid = jax.nn.gelu(acc)
    lhs = jnp.concatenate([jnp.where(is_v, 0.0, hid), jnp.where(is_v, hid, 0.0)], axis=1)
    o_ref[...] = jnp.dot(lhs.astype(BF16), w2_ref[...], preferred_element_type=F32)


def _compress_pages(cache_t, layer, n_phys, pe_t, w1p, w2p, pages=16):
    rows = 2 * N_KV * pages
    steps = n_phys // pages
    return pl.pallas_call(
        functools.partial(_compress_pages_kernel, rows=rows),
        grid=(steps,),
        in_specs=[
            pl.BlockSpec((rows * HEAD_DIM, LANES), lambda i: (layer * steps + i, 0)),
            pl.BlockSpec(pe_t.shape, lambda i: (0, 0, 0)),
            pl.BlockSpec(w1p.shape, lambda i: (0, 0, 0)),
            pl.BlockSpec(w2p.shape, lambda i: (0, 0)),
        ],
        out_specs=pl.BlockSpec((rows, LANES), lambda i: (i, 0)),
        out_shape=jax.ShapeDtypeStruct((n_phys * 2 * N_KV, LANES), F32),
        compiler_params=_params("parallel"),
    )(cache_t, pe_t, w1p, w2p)


def _compress_page_weights(pe_k, w1_k, w2_k, pe_v, w1_v, w2_v):
    eye = jnp.eye(2, dtype=F32)
    hid = w1_k.shape[1]

    def w1p(w1):
        w1d = w1.reshape(BLK, HEAD_DIM, hid).transpose(1, 0, 2)
        return jnp.einsum('drj,bc->dbrcj', w1d, eye)

    def w2p(w2):
        return jnp.einsum('je,bc->bjce', w2, eye)

    w1 = jnp.stack([w1p(w1_k), w1p(w1_v)], axis=1).reshape(HEAD_DIM, 4 * BLK, 2 * hid)
    w2 = jnp.stack([w2p(w2_k), w2p(w2_v)]).reshape(4 * hid, 2 * HEAD_DIM)
    pe_t = jnp.stack([jnp.tile(pe_k.T, (1, 2)), jnp.tile(pe_v.T, (1, 2))])
    return pe_t, w1.astype(BF16), w2.astype(BF16)


def _pool_kernel(u_ref, halo_ref, w_ref, s_ref, o_ref, *, tt, tiles_per_seq):
    i = pl.program_id(0)
    u = u_ref[...]
    halo = halo_ref[...]
    if tiles_per_seq is not None:
        halo = jnp.where(i % tiles_per_seq == 0, 0.0, halo)
        pos = (i % tiles_per_seq) * tt + lax.broadcasted_iota(jnp.int32, (tt, 1), 0)
    a = jnp.concatenate([halo, u], axis=0)
    gw = u.shape[1] // len(POOL_WINDOWS)
    for gi, w in enumerate(POOL_WINDOWS):
        b = a[:, gi * gw:(gi + 1) * gw]
        width = 1
        while width < w:
            b = b[:b.shape[0] - width] + b[width:]
            width *= 2
        lo = POOL_HALO + 1 - w
        win = b[lo:lo + tt]
        if tiles_per_seq is None:
            mean = win / float(w)
        else:
            mean = win / jnp.minimum(pos + 1, w).astype(F32)
        d = mean - u[:, gi * gw:(gi + 1) * gw]
        y = jnp.dot(d.astype(BF16), w_ref[gi], preferred_element_type=F32)
        o_ref[:, gi * gw:(gi + 1) * gw] = (y * s_ref[:, gi * gw:(gi + 1) * gw]).astype(o_ref.dtype)


def _pool_mix(u, w_grp, scale, tt, tiles_per_seq):
    m, width = u.shape
    hb = tt // POOL_HALO
    return pl.pallas_call(
        functools.partial(_pool_kernel, tt=tt, tiles_per_seq=tiles_per_seq),
        grid=(m // tt,),
        in_specs=[
            pl.BlockSpec((tt, width), lambda i: (i, 0)),
            pl.BlockSpec((POOL_HALO, width), lambda i: (jnp.maximum(i * hb - 1, 0), 0)),
            pl.BlockSpec(w_grp.shape, lambda i: (0, 0, 0)),
            pl.BlockSpec((1, width), lambda i: (0, 0)),
        ],
        out_specs=pl.BlockSpec((tt, width), lambda i: (i, 0)),
        out_shape=jax.ShapeDtypeStruct((m, width), BF16),
        compiler_params=_params("parallel"),
    )(u, u, w_grp, scale.reshape(1, width))


def _gated_branch_kernel(a_ref, b_ref, wa_ref, wb_ref, ga_ref, gb_ref, o_ref):
    a = jnp.dot(a_ref[...], wa_ref[...], preferred_element_type=F32)
    b = jnp.dot(b_ref[...], wb_ref[...], preferred_element_type=F32)
    o_ref[...] = (ga_ref[...].astype(F32) * a + gb_ref[...].astype(F32) * b).astype(o_ref.dtype)


def _gated_branches(a, b, wa, wb, gates, tm=1024, tn=512):
    m, k = a.shape
    n = wa.shape[1]
    nj = n // tn
    return pl.pallas_call(
        _gated_branch_kernel,
        grid=(m // tm, nj),
        in_specs=[
            pl.BlockSpec((tm, k), lambda i, j: (i, 0)),
            pl.BlockSpec((tm, k), lambda i, j: (i, 0)),
            pl.BlockSpec((k, tn), lambda i, j: (0, j)),
            pl.BlockSpec((k, tn), lambda i, j: (0, j)),
            pl.BlockSpec((tm, tn), lambda i, j: (i, j)),
            pl.BlockSpec((tm, tn), lambda i, j: (i, j + nj)),
        ],
        out_specs=pl.BlockSpec((tm, tn), lambda i, j: (i, j)),
        out_shape=jax.ShapeDtypeStruct((m, n), BF16),
        compiler_params=_params("parallel", "arbitrary"),
    )(a, b, wa, wb, gates, gates)


def _resid_matmul_kernel(x_ref, a_ref, w_ref, o_ref):
    o_ref[...] = x_ref[...] + jnp.dot(a_ref[...], w_ref[...], preferred_element_type=F32)


def _resid_matmul(x, a, w, tm=1024, tn=512):
    m, k = a.shape
    n = w.shape[1]
    return pl.pallas_call(
        _resid_matmul_kernel,
        grid=(m // tm, n // tn),
        in_specs=[
            pl.BlockSpec((tm, tn), lambda i, j: (i, j)),
            pl.BlockSpec((tm, k), lambda i, j: (i, 0)),
            pl.BlockSpec((k, tn), lambda i, j: (0, j)),
        ],
        out_specs=pl.BlockSpec((tm, tn), lambda i, j: (i, j)),
        out_shape=jax.ShapeDtypeStruct((m, n), F32),
        compiler_params=_params("parallel", "arbitrary"),
    )(x, a, w)


def _mlp_kernel(x_ref, g_ref, wu_ref, wd_ref, gf_ref, o_ref, h_ref, acc_ref, *, final_norm):
    f = pl.program_id(1)

    @pl.when(f == 0)
    def _():
        x = x_ref[...]
        ms = jnp.mean(x * x, axis=-1, keepdims=True)
        h_ref[...] = (x * lax.rsqrt(ms + EPS) * g_ref[...]).astype(BF16)
        acc_ref[...] = jnp.zeros_like(acc_ref)

    up = jnp.dot(h_ref[...], wu_ref[...], preferred_element_type=F32)
    act = jnp.square(jnp.maximum(up, 0.0)).astype(BF16)
    acc_ref[...] += jnp.dot(act, wd_ref[...], preferred_element_type=F32)

    @pl.when(f == pl.num_programs(1) - 1)
    def _():
        y = x_ref[...] + acc_ref[...]
        if final_norm:
            ms = jnp.mean(y * y, axis=-1, keepdims=True)
            y = y * lax.rsqrt(ms + EPS) * gf_ref[...]
        o_ref[...] = y


def _mlp(x, g, w_up, w_down, g_final, final_norm, tm=512, tf=512):
    m, d = x.shape
    dff = w_up.shape[1]
    return pl.pallas_call(
        functools.partial(_mlp_kernel, final_norm=final_norm),
        grid=(m // tm, dff // tf),
        in_specs=[
            pl.BlockSpec((tm, d), lambda i, f: (i, 0)),
            pl.BlockSpec((1, d), lambda i, f: (0, 0)),
            pl.BlockSpec((d, tf), lambda i, f: (0, f)),
            pl.BlockSpec((tf, d), lambda i, f: (f, 0)),
            pl.BlockSpec((1, d), lambda i, f: (0, 0)),
        ],
        out_specs=pl.BlockSpec((tm, d), lambda i, f: (i, 0)),
        out_shape=jax.ShapeDtypeStruct((m, d), F32),
        scratch_shapes=[pltpu.VMEM((tm, d), BF16), pltpu.VMEM((tm, d), F32)],
        compiler_params=_params("parallel", "arbitrary"),
    )(x, g.reshape(1, d), w_up, w_down, g_final.reshape(1, d))


def _shr(x, pow2):
    assert pow2 & (pow2 - 1) == 0
    return lax.shift_right_logical(x, pow2.bit_length() - 1)


def _masked_softmax(s, mask):
    sm = jnp.where(mask, s, NEG)
    e = jnp.exp(sm - jnp.max(sm, axis=-1, keepdims=True))
    return e / jnp.sum(e, axis=-1, keepdims=True)


def _select_blocks(imp, qpos, n_loop):
    r, nb = imp.shape
    bidx = lax.broadcasted_iota(jnp.int32, (r, nb), 1)
    cur = _shr(qpos, BLK)
    forced = (bidx == 0) | (bidx == cur) | (bidx == cur - 1)
    cand = bidx * BLK <= qpos
    score = jnp.where(cand, jnp.where(forced, FORCED, imp), NEG)
    rank = jnp.zeros((r, nb), F32)
    for j in range(n_loop):
        c = score[:, j:j + 1]
        beats = (c > score) | ((c == score) & (bidx > j))
        rank = rank + jnp.where(beats, 1.0, 0.0)
    return jnp.where(cand & (rank < TOP_N), 1.0, 0.0)


def _select_blocks_t(imp_t, qpos_t):
    nb, r = imp_t.shape
    bidx = lax.broadcasted_iota(jnp.int32, (nb, r), 0)
    cur = _shr(qpos_t, BLK)
    forced = (bidx == 0) | (bidx == cur) | (bidx == cur - 1)
    cand = bidx * BLK <= qpos_t
    score = jnp.where(cand, jnp.where(forced, FORCED, imp_t), NEG)
    n_chunks = nb // SUBLANES
    chunks = [score[SUBLANES * c:SUBLANES * (c + 1)] for c in range(n_chunks)]
    ranks = [jnp.zeros((SUBLANES, r), F32) for _ in range(n_chunks)]
    sub = lax.broadcasted_iota(jnp.int32, (SUBLANES, r), 0)
    for j in range(nb):
        cj = score[j:j + 1, :]
        for c in range(n_chunks):
            lo = SUBLANES * c
            ge = jnp.where(cj >= chunks[c], 1.0, 0.0)
            gt = jnp.where(cj > chunks[c], 1.0, 0.0)
            if lo > j:
                term = ge
            elif lo + SUBLANES - 1 < j:
                term = gt
            else:
                term = jnp.where(sub > j - lo, ge, gt)
            ranks[c] = ranks[c] + term
    rank = jnp.concatenate(ranks, axis=0)
    return jnp.where(cand & (rank < TOP_N), 1.0, 0.0)


def _block_expand(n_blocks, k0, n_keys):
    b = lax.broadcasted_iota(jnp.int32, (n_blocks, n_keys), 0)
    k = lax.broadcasted_iota(jnp.int32, (n_blocks, n_keys), 1) + k0
    return jnp.where(_shr(k, BLK) == b, 1.0, 0.0).astype(BF16)


def _nsa_prompt_kernel(q_ref, kc_ref, vc_ref, ks_ref, vs_ref, kw_ref, vw_ref, g_ref,
                       o_ref, *, seq):
    i = pl.program_id(2)
    rows = GROUP * Q_TILE
    qt = q_ref[...]
    q = jnp.concatenate([qt[:, g * HEAD_DIM:(g + 1) * HEAD_DIM] for g in range(GROUP)], axis=0)
    tok = lax.broadcasted_iota(jnp.int32, (Q_TILE, 1), 0)
    qpos1 = i * Q_TILE + tok
    qpos = jnp.concatenate([qpos1] * GROUP, axis=0)
    nb = seq // BLK
    kc = kc_ref[0, 0]

    s = lax.dot_general(q, kc, _NT, preferred_element_type=F32)
    bidx = lax.broadcasted_iota(jnp.int32, (1, nb), 1)
    cmask = bidx * BLK + (BLK - 1) <= qpos
    p = jnp.where(cmask, _masked_softmax(s, cmask), 0.0)
    o_cmp = jnp.dot(p.astype(BF16), vc_ref[0, 0], preferred_element_type=F32)

    s_t = lax.dot_general(kc, q, _NT, preferred_element_type=F32)
    qpos_t = i * Q_TILE + (lax.broadcasted_iota(jnp.int32, (1, rows), 1) & (Q_TILE - 1))
    cmask_t = lax.broadcasted_iota(jnp.int32, (nb, 1), 0) * BLK + (BLK - 1) <= qpos_t
    sm_t = jnp.where(cmask_t, s_t, NEG)
    e_t = jnp.exp(sm_t - jnp.max(sm_t, axis=0, keepdims=True))
    p_t = jnp.where(cmask_t, e_t / jnp.sum(e_t, axis=0, keepdims=True), 0.0)
    imp_t = p_t[:, 0:Q_TILE]
    for g in range(1, GROUP):
        imp_t = imp_t + p_t[:, g * Q_TILE:(g + 1) * Q_TILE]
    sel = _select_blocks_t(imp_t, qpos_t[:, 0:Q_TILE]).T

    own = _shr(bidx, Q_TILE // BLK) == i
    block_bias = jnp.where((sel > 0.5) & jnp.logical_not(own), 0.0, NEG).astype(BF16)
    q_aug = jnp.concatenate([q, jnp.concatenate([block_bias] * GROUP, axis=0)], axis=1)

    def online_step(carry, scores, v):
        m, l, acc = carry
        m_new = jnp.maximum(m, jnp.max(scores, axis=-1, keepdims=True))
        alpha = jnp.exp(m - m_new)
        pj = jnp.exp(scores - m_new)
        l = alpha * l + jnp.sum(pj, axis=-1, keepdims=True)
        acc = alpha * acc + jnp.dot(pj.astype(BF16), v, preferred_element_type=F32)
        return m_new, l, acc

    def body(j, carry):
        k0 = pl.multiple_of(j * KEY_TILE, KEY_TILE)
        k = ks_ref[0, 0, pl.ds(k0, KEY_TILE), :]
        v = vs_ref[0, 0, pl.ds(k0, KEY_TILE), :]
        return online_step(carry, lax.dot_general(q_aug, k, _NT, preferred_element_type=F32), v)

    n_tiles = lax.div(i * Q_TILE + KEY_TILE - 1, KEY_TILE)
    init = (jnp.full((rows, 1), NEG, F32), jnp.zeros((rows, 1), F32),
            jnp.zeros((rows, HEAD_DIM), F32))
    carry = lax.fori_loop(0, n_tiles, body, init)

    d0 = pl.multiple_of(i * Q_TILE, Q_TILE)
    kd = ks_ref[0, 0, pl.ds(d0, Q_TILE), :]
    vd = vs_ref[0, 0, pl.ds(d0, Q_TILE), :]
    causal = d0 + lax.broadcasted_iota(jnp.int32, (1, Q_TILE), 1) <= qpos
    qd = jnp.concatenate([q, jnp.zeros((rows, nb), BF16)], axis=1)
    sd = lax.dot_general(qd, kd, _NT, preferred_element_type=F32)
    _, l, acc = online_step(carry, jnp.where(causal, sd, NEG), vd)
    o_sel = acc / l

    span = WINDOW + Q_TILE
    start = pl.multiple_of(jnp.maximum(i * Q_TILE - WINDOW, 0), Q_TILE)
    kw = kw_ref[0, 0, pl.ds(start, span), :]
    vw = vw_ref[0, 0, pl.ds(start, span), :]
    sw = lax.dot_general(q, kw, _NT, preferred_element_type=F32)
    wpos = start + lax.broadcasted_iota(jnp.int32, (1, span), 1)
    wbias = jnp.where((wpos <= qpos1) & (wpos >= qpos1 - WINDOW), 0.0, NEG)
    sw = (sw.reshape(GROUP, Q_TILE, span) + wbias[None]).reshape(rows, span)
    ew = jnp.exp(sw - jnp.max(sw, axis=-1, keepdims=True))
    pw = ew / jnp.sum(ew, axis=-1, keepdims=True)
    o_win = jnp.dot(pw.astype(BF16), vw, preferred_element_type=F32)

    gt = g_ref[...]
    outs = []
    for g in range(GROUP):
        sl = slice(g * Q_TILE, (g + 1) * Q_TILE)
        outs.append(gt[:, g:g + 1] * o_cmp[sl]
                    + gt[:, GROUP + g:GROUP + g + 1] * o_sel[sl]
                    + gt[:, 2 * GROUP + g:2 * GROUP + g + 1] * o_win[sl])
    o_ref[...] = jnp.concatenate(outs, axis=1).astype(o_ref.dtype)


def _nsa_prompt(q, kc, vc, ks, vs, kw, vw, gates, n, seq):
    nq = seq // Q_TILE
    nb = seq // BLK
    onehot = (jnp.arange(seq)[:, None] // BLK == jnp.arange(nb)[None, :]).astype(BF16)
    ks = jnp.concatenate([ks, jnp.broadcast_to(onehot, ks.shape[:2] + onehot.shape)], axis=-1)
    row = lambda b, h, i: (b * nq + i, h)
    head = lambda b, h, i: (b, h, 0, 0)
    kv_spec = pl.BlockSpec((1, 1, seq, HEAD_DIM), head)
    c_spec = pl.BlockSpec((1, 1, nb, HEAD_DIM), head)
    return pl.pallas_call(
        functools.partial(_nsa_prompt_kernel, seq=seq),
        grid=(n, N_KV, nq),
        in_specs=[
            pl.BlockSpec((Q_TILE, GROUP * HEAD_DIM), row),
            c_spec, c_spec, pl.BlockSpec((1, 1, seq, HEAD_DIM + nb), head), kv_spec, kv_spec, kv_spec,
            pl.BlockSpec((Q_TILE, LANES), row),
        ],
        out_specs=pl.BlockSpec((Q_TILE, GROUP * HEAD_DIM), row),
        out_shape=jax.ShapeDtypeStruct((n * seq, N_KV * GROUP * HEAD_DIM), BF16),
        compiler_params=_params("parallel", "parallel", "arbitrary"),
    )(q, kc, vc, ks, vs, kw, vw, gates)


def _nsa_sample_kernel(pt_ref, q_ref, g_ref, *refs, n_pages, page, dec, wb):
    del pt_ref
    cmp_pages = refs[:n_pages]
    cmp_new = refs[n_pages]
    sel_pages = refs[n_pages + 1:2 * n_pages + 1]
    sel_new = refs[2 * n_pages + 1]
    win_ref, win_new, o_ref, wo_ref, kc_ref, k_ref, v_ref, kw_ref, vw_ref = refs[2 * n_pages + 2:]
    past = n_pages * page
    rows = N_KV * GROUP * dec
    q = q_ref[0]
    ridx = lax.broadcasted_iota(jnp.int32, (rows, 1), 0)
    qpos = past + (ridx & (dec - 1))

    bpp = page // BLK
    kc_ref[...] = jnp.zeros_like(kc_ref)
    for p_i in range(n_pages):
        kc_ref[bpp * p_i:bpp * (p_i + 1), :] = cmp_pages[p_i][0]
    kc_ref[bpp * n_pages:bpp * n_pages + 1, :] = cmp_new[0, 0:1, :]
    nbp = kc_ref.shape[0]
    s = lax.dot_general(q, kc_ref[:, :KV_W].astype(BF16), _NT, preferred_element_type=F32)
    bidx = lax.broadcasted_iota(jnp.int32, (1, nbp), 1)
    cmask = bidx * BLK + (BLK - 1) <= qpos
    p = jnp.where(cmask, _masked_softmax(s, cmask), 0.0)
    o_cmp = jnp.dot(p.astype(BF16), kc_ref[:, KV_W:].astype(BF16), preferred_element_type=F32)

    imps = []
    for h in range(N_KV):
        base = h * GROUP * dec
        acc = p[base:base + dec]
        for g in range(1, GROUP):
            acc = acc + p[base + g * dec:base + (g + 1) * dec]
        imps.append(acc)
    imp = jnp.concatenate(imps, axis=0)
    qpos_ht = past + (lax.broadcasted_iota(jnp.int32, (N_KV * dec, 1), 0) & (dec - 1))
    sel = _select_blocks(imp, qpos_ht, bpp * n_pages + 1)
    sel_rows = jnp.concatenate(
        [sel[h * dec:(h + 1) * dec] for h in range(N_KV) for _ in range(GROUP)], axis=0)

    n_keys = past + page
    for p_i in range(n_pages + 1):
        pg = sel_pages[p_i][0] if p_i < n_pages else sel_new[0]
        k_ref[:, p_i * page:(p_i + 1) * page] = pg[:KV_W].astype(BF16)
        v_ref[:, p_i * page:(p_i + 1) * page] = pg[KV_W:].astype(BF16)
    ss = jnp.dot(q, k_ref[...], preferred_element_type=F32)
    key_sel = jnp.dot(sel_rows.astype(BF16), _block_expand(nbp, 0, n_keys),
                      preferred_element_type=F32)
    kpos = lax.broadcasted_iota(jnp.int32, (1, n_keys), 1)
    ps = _masked_softmax(ss, (key_sel > 0.5) & (kpos <= qpos))
    o_sel = lax.dot_general(ps.astype(BF16), v_ref[...], _NT, preferred_element_type=F32)

    wn = win_ref[0]
    nw = win_new[0]
    kw_ref[:, 0:wb] = wn[:KV_W].astype(BF16)
    vw_ref[:, 0:wb] = wn[KV_W:].astype(BF16)
    kw_ref[:, wb:wb + page] = nw[:KV_W].astype(BF16)
    vw_ref[:, wb:wb + page] = nw[KV_W:].astype(BF16)
    sw = jnp.dot(q, kw_ref[...], preferred_element_type=F32)
    wpos = past - wb + lax.broadcasted_iota(jnp.int32, (1, wb + page), 1)
    pw = _masked_softmax(sw, (wpos <= qpos) & (wpos >= qpos - WINDOW))
    o_win = lax.dot_general(pw.astype(BF16), vw_ref[...], _NT, preferred_element_type=F32)

    gt = g_ref[0]
    o = gt[:, 0:1] * o_cmp + gt[:, 1:2] * o_sel + gt[:, 2:3] * o_win
    col = lax.broadcasted_iota(jnp.int32, (1, KV_W), 1)
    o = jnp.where(_shr(ridx, GROUP * dec) == _shr(col, HEAD_DIM), o, 0.0)
    hr = GROUP * dec
    out = o[0:hr]
    for h in range(1, N_KV):
        out = out + o[h * hr:(h + 1) * hr]
    o_ref[0] = out

    shifted = pltpu.roll(wn, wb - dec, 1)
    tail = pltpu.roll(nw, page - dec, 1)
    lane = lax.broadcasted_iota(jnp.int32, (1, page), 1)
    wo_ref[0, :, 0:wb - page] = shifted[:, 0:wb - page]
    wo_ref[0, :, wb - page:wb] = jnp.where(lane >= page - dec, tail, shifted[:, wb - page:wb])


def _nsa_sample(page_table, layer, qbd, gates, cmp_phys, cmp_new, sel_cache_t, sel_new,
                win_cache_t, win_new):
    nseq, n_pages = page_table.shape
    n_phys = cmp_phys.shape[0]
    page = sel_cache_t.shape[2]
    rows = qbd.shape[1]
    dec = rows // (N_KV * GROUP)
    wb = win_cache_t.shape[2]
    bpp = page // BLK
    nbp = LANES
    assert bpp * n_pages + 1 <= nbp and wb == WINDOW and page == LANES and dec <= SUBLANES

    def page_map(p_i, base):
        return lambda b, pt: (base + pt[b * n_pages + p_i], 0, 0)

    seq_map = lambda b, pt: (b, 0, 0)
    in_specs = [
        pl.BlockSpec((1, rows, KV_W), seq_map),
        pl.BlockSpec((1, rows, 8), seq_map),
    ]
    in_specs += [pl.BlockSpec((1, bpp, 2 * KV_W), page_map(p_i, 0)) for p_i in range(n_pages)]
    in_specs += [pl.BlockSpec((1, 8, 2 * KV_W), seq_map)]
    in_specs += [pl.BlockSpec((1, 2 * KV_W, page), page_map(p_i, layer * n_phys))
                 for p_i in range(n_pages)]
    in_specs += [
        pl.BlockSpec((1, 2 * KV_W, page), seq_map),
        pl.BlockSpec((1, 2 * KV_W, wb), lambda b, pt: (layer * nseq + b, 0, 0)),
        pl.BlockSpec((1, 2 * KV_W, page), seq_map),
    ]
    n_keys = (n_pages + 1) * page
    grid_spec = pltpu.PrefetchScalarGridSpec(
        num_scalar_prefetch=1,
        grid=(nseq,),
        in_specs=in_specs,
        out_specs=[
            pl.BlockSpec((1, GROUP * dec, KV_W), seq_map),
            pl.BlockSpec((1, 2 * KV_W, wb), seq_map),
        ],
        scratch_shapes=[
            pltpu.VMEM((nbp, 2 * KV_W), F32),
            pltpu.VMEM((KV_W, n_keys), BF16),
            pltpu.VMEM((KV_W, n_keys), BF16),
            pltpu.VMEM((KV_W, wb + page), BF16),
            pltpu.VMEM((KV_W, wb + page), BF16),
        ],
    )
    return pl.pallas_call(
        functools.partial(_nsa_sample_kernel, n_pages=n_pages, page=page, dec=dec, wb=wb),
        grid_spec=grid_spec,
        out_shape=[
            jax.ShapeDtypeStruct((nseq, GROUP * dec, KV_W), F32),
            jax.ShapeDtypeStruct((nseq, 2 * KV_W, wb), F32),
        ],
        compiler_params=_params("arbitrary"),
    )(page_table.reshape(-1), qbd, gates, *([cmp_phys] * n_pages), cmp_new,
      *([sel_cache_t] * n_pages), sel_new, win_cache_t, win_new)


def _heads_major(a, n, seq):
    return a.reshape(n, seq, N_KV, HEAD_DIM).transpose(0, 2, 1, 3).astype(BF16)


def _pad_rows(a, rows):
    return jnp.pad(a, ((0, 0), (0, rows - a.shape[1]), (0, 0)))


def _channel_major(cache):
    depth, a, rows = cache.shape[:3]
    return cache.transpose(0, 1, 3, 4, 5, 2).reshape(depth * a, 2 * KV_W, rows)


def kernel(x_prompt, x_sample, cache_cmp_kv, cache_sel_kv, cache_win_kv, state_pool, page_table, g_mix, w_in, pe_ck, w_ck1, w_ck2, pe_cv, w_cv1, w_cv2, w_pool_grp, pool_scale, w_br_nsa, w_br_pool, w_o, g_mlp, w_up, w_down, g_final):
    n, seq, d = x_prompt.shape
    nseq, dec, _ = x_sample.shape
    depth = w_in.shape[0]
    n_phys, page = cache_cmp_kv.shape[1], cache_cmp_kv.shape[2]
    wb = cache_win_kv.shape[2]
    mp, ms = n * seq, nseq * dec
    nsa_w = N_KV * GROUP * HEAD_DIM
    kv_cols = 2 * KV_W
    n_gate = 3 * N_KV * GROUP
    pool_w = state_pool.shape[-1]
    pool_state = state_pool.shape[2]
    c_q, c_kv, c_ng = nsa_w, nsa_w + 3 * kv_cols, nsa_w + 3 * kv_cols + n_gate
    c_u = c_ng + pool_w

    x = jnp.concatenate([x_prompt.reshape(mp, d), x_sample.reshape(ms, d)], axis=0)
    kv_shape = (2, N_KV, HEAD_DIM)
    outs = [[] for _ in range(8)]
    cmp_cache_t = _channel_major(cache_cmp_kv).reshape(depth * n_phys * kv_cols, page)
    sel_cache_t = _channel_major(cache_sel_kv)
    win_cache_t = _channel_major(cache_win_kv)

    for l in range(depth):
        w = w_in[l]
        w_q = (w[:, :c_q] * HEAD_DIM ** -0.5).astype(BF16)
        w_kv = w[:, c_q:c_kv].astype(BF16)
        w_ng = w[:, c_kv:c_ng].reshape(d, 3, N_KV, GROUP).transpose(0, 2, 1, 3).reshape(d, N_KV, 3 * GROUP)
        w_ng = jnp.pad(w_ng, ((0, 0), (0, 0), (0, LANES - 3 * GROUP))).reshape(d, N_KV * LANES).astype(BF16)
        w_u = w[:, c_ng:c_u].astype(BF16)
        w_mg = w[:, c_u:].astype(BF16)

        zkv = _norm_matmul(x, g_mix[l], w_kv, F32)
        q = _norm_matmul(x, g_mix[l], w_q, BF16)
        u = _norm_matmul(x, g_mix[l], w_u, F32)
        ng = _norm_matmul(x, g_mix[l], w_ng, F32, sigmoid=True)
        mg = _norm_matmul(x, g_mix[l], w_mg, BF16, sigmoid=True)

        cmp_w = (pe_ck[l], w_ck1[l], w_ck2[l], pe_cv[l], w_cv1[l], w_cv2[l])
        cw = _compress_weights(*cmp_w)

        cmp_p = _compress(zkv, mp // BLK, *cw)
        kc_p = _heads_major(cmp_p[:, :KV_W], n, seq // BLK)
        vc_p = _heads_major(cmp_p[:, KV_W:], n, seq // BLK)
        zp = zkv[:mp]
        hm = lambda c: _heads_major(zp[:, c * KV_W:(c + 1) * KV_W], n, seq)
        o_nsa_p = _nsa_prompt(q, kc_p, vc_p, hm(2), hm(3), hm(4), hm(5), ng, n, seq)

        zs = zkv[mp:].reshape(nseq, dec, 3 * kv_cols)
        new_c = _pad_rows(zs[:, :, :kv_cols], page)
        lanes_t = lambda a: jnp.pad(a.transpose(0, 2, 1), ((0, 0), (0, 0), (0, page - dec)))
        new_s = lanes_t(zs[:, :, kv_cols:2 * kv_cols])
        new_w = lanes_t(zs[:, :, 2 * kv_cols:])
        bpp = page // BLK
        cmp_phys = _compress_pages(cmp_cache_t, l, n_phys, *_compress_page_weights(*cmp_w))
        cmp_phys = cmp_phys.reshape(n_phys, 2, N_KV, bpp, HEAD_DIM).transpose(0, 3, 1, 2, 4)
        cmp_phys = cmp_phys.reshape(n_phys, bpp, kv_cols)
        cmp_new = _compress(new_c.reshape(nseq * page, kv_cols), nseq * bpp, *cw)
        cmp_new = _pad_rows(cmp_new.reshape(nseq, bpp, kv_cols)[:, :1], 8)

        qs = q[mp:].reshape(nseq, dec, N_KV, GROUP, HEAD_DIM)
        eye = jnp.eye(N_KV, dtype=BF16)
        qbd = jnp.einsum('bthgd,hk->bhgtkd', qs, eye).reshape(nseq, N_KV * GROUP * dec, KV_W)
        gs = ng[mp:].reshape(nseq, dec, N_KV, LANES)[..., :3 * GROUP].reshape(nseq, dec, N_KV, 3, GROUP)
        gs = gs.transpose(0, 2, 4, 1, 3).reshape(nseq, N_KV * GROUP * dec, 3)
        gs = jnp.pad(gs, ((0, 0), (0, 0), (0, 5)))
        o_s, win_t = _nsa_sample(page_table, l, qbd, gs, cmp_phys, cmp_new, sel_cache_t, new_s,
                                 win_cache_t, new_w)
        o_nsa_s = o_s.reshape(nseq, GROUP, dec, N_KV, HEAD_DIM).transpose(0, 2, 3, 1, 4)
        o_nsa = jnp.concatenate([o_nsa_p, o_nsa_s.reshape(ms, nsa_w).astype(BF16)], axis=0)

        w_grp = w_pool_grp[l].astype(BF16)
        u_p, u_s = u[:mp], u[mp:].reshape(nseq, dec, pool_w)
        o_pool_p = _pool_mix(u_p, w_grp, pool_scale[l], 512, seq // 512)
        hist = jnp.pad(state_pool[l], ((0, 0), (POOL_HALO - pool_state, 0), (0, 0)))
        xx_s = jnp.concatenate([hist, u_s], axis=1)
        rows_s = POOL_HALO + dec
        o_pool_s = _pool_mix(xx_s.reshape(nseq * rows_s, pool_w), w_grp, pool_scale[l],
                             16 * rows_s, None)
        o_pool_s = o_pool_s.reshape(nseq, rows_s, pool_w)[:, POOL_HALO:].reshape(ms, pool_w)
        o_pool = jnp.concatenate([o_pool_p, o_pool_s], axis=0)

        m = _gated_branches(o_nsa, o_pool, w_br_nsa[l].astype(BF16), w_br_pool[l].astype(BF16), mg)
        x1 = _resid_matmul(x, m, w_o[l].astype(BF16))
        x = _mlp(x1, g_mlp[l], w_up[l].astype(BF16), w_down[l].astype(BF16), g_final,
                 final_norm=(l == depth - 1))

        zp6 = zp.reshape(n, seq, 3, *kv_shape)
        outs[0].append(zp6[:, :, 0])
        outs[1].append(zp6[:, :, 1])
        outs[2].append(zp6[:, -min(WINDOW, seq):, 2])
        outs[3].append(u_p.reshape(n, seq, pool_w)[:, -pool_state:])
        zs6 = zs.reshape(nseq, dec, 3, *kv_shape)
        outs[4].append(zs6[:, :, 0])
        outs[5].append(zs6[:, :, 1])
        outs[6].append(win_t.reshape(nseq, *kv_shape, wb).transpose(0, 4, 1, 2, 3))
        outs[7].append(xx_s[:, -pool_state:])

    y_prompt = x[:mp].reshape(n, seq, d)
    y_sample = x[mp:].reshape(nseq, dec, d)
    return (y_prompt, y_sample) + tuple(jnp.stack(o) for o in outs)
```

```python
import functools

import jax
import jax.numpy as jnp
from jax import lax
from jax.experimental import pallas as pl
from jax.experimental.pallas import tpu as pltpu

F32 = jnp.float32
BF16 = jnp.bfloat16

N_KV = 4
GROUP = 4
HEAD_DIM = 64
KV_W = N_KV * HEAD_DIM
BLK = 64
TOP_N = 16
WINDOW = 512
Q_TILE = 128
KEY_TILE = 512
POOL_WINDOWS = (2, 4, 8, 16)
POOL_HALO = 16
EPS = 1e-6
NEG = -1e30
FORCED = 1e4
LANES = 128
SUBLANES = 8
VMEM_LIMIT = 56 * 1024 * 1024

_NT = (((1,), (1,)), ((), ()))


def _params(*sem):
    return pltpu.CompilerParams(dimension_semantics=sem, vmem_limit_bytes=VMEM_LIMIT)


def _norm_matmul_kernel(x_ref, g_ref, w_ref, o_ref, xn_ref, *, sigmoid):
    @pl.when(pl.program_id(1) == 0)
    def _():
        x = x_ref[...]
        ms = jnp.mean(x * x, axis=-1, keepdims=True)
        xn_ref[...] = (x * lax.rsqrt(ms + EPS) * g_ref[...]).astype(BF16)

    z = jnp.dot(xn_ref[...], w_ref[...], preferred_element_type=F32)
    if sigmoid:
        z = jax.nn.sigmoid(z)
    o_ref[...] = z.astype(o_ref.dtype)


def _norm_matmul(x, g, w, out_dtype, sigmoid=False, tm=1024, tn=512):
    m, d = x.shape
    n = w.shape[1]
    tm, tn = min(tm, m), min(tn, n)
    return pl.pallas_call(
        functools.partial(_norm_matmul_kernel, sigmoid=sigmoid),
        grid=(m // tm, n // tn),
        in_specs=[
            pl.BlockSpec((tm, d), lambda i, j: (i, 0)),
            pl.BlockSpec((1, d), lambda i, j: (0, 0)),
            pl.BlockSpec((d, tn), lambda i, j: (0, j)),
        ],
        out_specs=pl.BlockSpec((tm, tn), lambda i, j: (i, j)),
        out_shape=jax.ShapeDtypeStruct((m, n), out_dtype),
        scratch_shapes=[pltpu.VMEM((tm, d), BF16)],
        compiler_params=_params("parallel", "arbitrary"),
    )(x, g.reshape(1, d), w)


def _compress_kernel(x_ref, pe_ref, w1_ref, w2_ref, o_ref, *, tb):
    acc = jnp.zeros((tb, 2 * LANES), F32)
    for r in range(BLK):
        xr = x_ref[pl.ds(r, tb, stride=BLK), :] + pe_ref[0, r:r + 1, :]
        acc = acc + jnp.dot(xr.astype(BF16), w1_ref[0, r], preferred_element_type=F32)
    hid = jax.nn.gelu(acc)
    o_ref[...] = jnp.dot(hid.astype(BF16), w2_ref[0], preferred_element_type=F32)


def _compress(rows, n_blocks, pe2, w1bd, w2bd, tb=128):
    return pl.pallas_call(
        functools.partial(_compress_kernel, tb=tb),
        grid=(n_blocks // tb, 4),
        in_specs=[
            pl.BlockSpec((tb * BLK, LANES), lambda i, c: (i, c)),
            pl.BlockSpec((1, BLK, LANES), lambda i, c: (c // 2, 0, 0)),
            pl.BlockSpec((1, BLK, LANES, 2 * LANES), lambda i, c: (c // 2, 0, 0, 0)),
            pl.BlockSpec((1, 2 * LANES, LANES), lambda i, c: (c // 2, 0, 0)),
        ],
        out_specs=pl.BlockSpec((tb, LANES), lambda i, c: (i, c)),
        out_shape=jax.ShapeDtypeStruct((n_blocks, 4 * LANES), F32),
        compiler_params=_params("parallel", "arbitrary"),
    )(rows, pe2, w1bd, w2bd)


def _compress_weights(pe_k, w1_k, w2_k, pe_v, w1_v, w2_v):
    def one(pe, w1, w2):
        hid = w1.shape[1]
        w1r = w1.reshape(BLK, HEAD_DIM, hid)
        z = jnp.zeros_like(w1r)
        w1bd = jnp.concatenate(
            [jnp.concatenate([w1r, z], axis=2), jnp.concatenate([z, w1r], axis=2)], axis=1)
        z2 = jnp.zeros_like(w2)
        w2bd = jnp.concatenate(
            [jnp.concatenate([w2, z2], axis=1), jnp.concatenate([z2, w2], axis=1)], axis=0)
        return jnp.concatenate([pe, pe], axis=1), w1bd.astype(BF16), w2bd.astype(BF16)

    k, v = one(pe_k, w1_k, w2_k), one(pe_v, w1_v, w2_v)
    return tuple(jnp.stack([a, b]) for a, b in zip(k, v))


def _compress_pages_kernel(x_ref, pe_ref, w1_ref, w2_ref, o_ref, *, rows):
    ridx = lax.broadcasted_iota(jnp.int32, (rows, 1), 0)
    is_v = (_shr(ridx, N_KV) & 1) == 1
    acc = jnp.zeros((rows, 2 * LANES), F32)
    for d in range(HEAD_DIM):
        a = x_ref[pl.ds(d, rows, stride=HEAD_DIM), :]
        ak = jnp.where(is_v, 0.0, a + pe_ref[0, d:d + 1, :])
        av = jnp.where(is_v, a + pe_ref[1, d:d + 1, :], 0.0)
        lhs = jnp.concatenate([ak, av], axis=1).astype(BF16)
        acc = acc + jnp.dot(lhs, w1_ref[d], preferred_element_type=F32)
    hid = jax.nn.gelu(acc)
    lhs = jnp.concatenate([jnp.where(is_v, 0.0, hid), jnp.where(is_v, hid, 0.0)], axis=1)
    o_ref[...] = jnp.dot(lhs.astype(BF16), w2_ref[...], preferred_element_type=F32)


def _compress_pages(cache_t, layer, n_phys, pe_t, w1p, w2p, pages=16):
    rows = 2 * N_KV * pages
    steps = n_phys // pages
    return pl.pallas_call(
        functools.partial(_compress_pages_kernel, rows=rows),
        grid=(steps,),
        in_specs=[
            pl.BlockSpec((rows * HEAD_DIM, LANES), lambda i: (layer * steps + i, 0)),
            pl.BlockSpec(pe_t.shape, lambda i: (0, 0, 0)),
            pl.BlockSpec(w1p.shape, lambda i: (0, 0, 0)),
            pl.BlockSpec(w2p.shape, lambda i: (0, 0)),
        ],
        out_specs=pl.BlockSpec((rows, LANES), lambda i: (i, 0)),
        out_shape=jax.ShapeDtypeStruct((n_phys * 2 * N_KV, LANES), F32),
        compiler_params=_params("parallel"),
    )(cache_t, pe_t, w1p, w2p)


def _compress_page_weights(pe_k, w1_k, w2_k, pe_v, w1_v, w2_v):
    eye = jnp.eye(2, dtype=F32)
    hid = w1_k.shape[1]

    def w1p(w1):
        w1d = w1.reshape(BLK, HEAD_DIM, hid).transpose(1, 0, 2)
        return jnp.einsum('drj,bc->dbrcj', w1d, eye)

    def w2p(w2):
        return jnp.einsum('je,bc->bjce', w2, eye)

    w1 = jnp.stack([w1p(w1_k), w1p(w1_v)], axis=1).reshape(HEAD_DIM, 4 * BLK, 2 * hid)
    w2 = jnp.stack([w2p(w2_k), w2p(w2_v)]).reshape(4 * hid, 2 * HEAD_DIM)
    pe_t = jnp.stack([jnp.tile(pe_k.T, (1, 2)), jnp.tile(pe_v.T, (1, 2))])
    return pe_t, w1.astype(BF16), w2.astype(BF16)


def _pool_kernel(u_ref, halo_ref, w_ref, s_ref, o_ref, *, tt, tiles_per_seq):
    i = pl.program_id(0)
    u = u_ref[...]
    halo = halo_ref[...]
    if tiles_per_seq is not None:
        halo = jnp.where(i % tiles_per_seq == 0, 0.0, halo)
        pos = (i % tiles_per_seq) * tt + lax.broadcasted_iota(jnp.int32, (tt, 1), 0)
    a = jnp.concatenate([halo, u], axis=0)
    gw = u.shape[1] // len(POOL_WINDOWS)
    for gi, w in enumerate(POOL_WINDOWS):
        b = a[:, gi * gw:(gi + 1) * gw]
        width = 1
        while width < w:
            b = b[:b.shape[0] - width] + b[width:]
            width *= 2
        lo = POOL_HALO + 1 - w
        win = b[lo:lo + tt]
        if tiles_per_seq is None:
            mean = win / float(w)
        else:
            mean = win / jnp.minimum(pos + 1, w).astype(F32)
        d = mean - u[:, gi * gw:(gi + 1) * gw]
        y = jnp.dot(d.astype(BF16), w_ref[gi], preferred_element_type=F32)
        o_ref[:, gi * gw:(gi + 1) * gw] = (y * s_ref[:, gi * gw:(gi + 1) * gw]).astype(o_ref.dtype)


def _pool_mix(u, w_grp, scale, tt, tiles_per_seq):
    m, width = u.shape
    hb = tt // POOL_HALO
    return pl.pallas_call(
        functools.partial(_pool_kernel, tt=tt, tiles_per_seq=tiles_per_seq),
        grid=(m // tt,),
        in_specs=[
            pl.BlockSpec((tt, width), lambda i: (i, 0)),
            pl.BlockSpec((POOL_HALO, width), lambda i: (jnp.maximum(i * hb - 1, 0), 0)),
            pl.BlockSpec(w_grp.shape, lambda i: (0, 0, 0)),
            pl.BlockSpec((1, width), lambda i: (0, 0)),
        ],
        out_specs=pl.BlockSpec((tt, width), lambda i: (i, 0)),
        out_shape=jax.ShapeDtypeStruct((m, width), BF16),
        compiler_params=_params("parallel"),
    )(u, u, w_grp, scale.reshape(1, width))


def _gated_branch_kernel(a_ref, b_ref, wa_ref, wb_ref, ga_ref, gb_ref, o_ref):
    a = jnp.dot(a_ref[...], wa_ref[...], preferred_element_type=F32)
    b = jnp.dot(b_ref[...], wb_ref[...], preferred_element_type=F32)
    o_ref[...] = (ga_ref[...].astype(F32) * a + gb_ref[...].astype(F32) * b).astype(o_ref.dtype)


def _gated_branches(a, b, wa, wb, gates, tm=1024, tn=512):
    m, k = a.shape
    n = wa.shape[1]
    nj = n // tn
    return pl.pallas_call(
        _gated_branch_kernel,
        grid=(m // tm, nj),
        in_specs=[
            pl.BlockSpec((tm, k), lambda i, j: (i, 0)),
            pl.BlockSpec((tm, k), lambda i, j: (i, 0)),
            pl.BlockSpec((k, tn), lambda i, j: (0, j)),
            pl.BlockSpec((k, tn), lambda i, j: (0, j)),
            pl.BlockSpec((tm, tn), lambda i, j: (i, j)),
            pl.BlockSpec((tm, tn), lambda i, j: (i, j + nj)),
        ],
        out_specs=pl.BlockSpec((tm, tn), lambda i, j: (i, j)),
        out_shape=jax.ShapeDtypeStruct((m, n), BF16),
        compiler_params=_params("parallel", "arbitrary"),
    )(a, b, wa, wb, gates, gates)


def _resid_matmul_kernel(x_ref, a_ref, w_ref, o_ref):
    o_ref[...] = x_ref[...] + jnp.dot(a_ref[...], w_ref[...], preferred_element_type=F32)


def _resid_matmul(x, a, w, tm=1024, tn=512):
    m, k = a.shape
    n = w.shape[1]
    return pl.pallas_call(
        _resid_matmul_kernel,
        grid=(m // tm, n // tn),
        in_specs=[
            pl.BlockSpec((tm, tn), lambda i, j: (i, j)),
            pl.BlockSpec((tm, k), lambda i, j: (i, 0)),
            pl.BlockSpec((k, tn), lambda i, j: (0, j)),
        ],
        out_specs=pl.BlockSpec((tm, tn), lambda i, j: (i, j)),
        out_shape=jax.ShapeDtypeStruct((m, n), F32),
        compiler_params=_params("parallel", "arbitrary"),
    )(x, a, w)


def _mlp_kernel(x_ref, g_ref, wu_ref, wd_ref, gf_ref, o_ref, h_ref, acc_ref, *, final_norm):
    f = pl.program_id(1)

    @pl.when(f == 0)
    def _():
        x = x_ref[...]
        ms = jnp.mean(x * x, axis=-1, keepdims=True)
        h_ref[...] = (x * lax.rsqrt(ms + EPS) * g_ref[...]).astype(BF16)
        acc_ref[...] = jnp.zeros_like(acc_ref)

    up = jnp.dot(h_ref[...], wu_ref[...], preferred_element_type=F32)
    act = jnp.square(jnp.maximum(up, 0.0)).astype(BF16)
    acc_ref[...] += jnp.dot(act, wd_ref[...], preferred_element_type=F32)

    @pl.when(f == pl.num_programs(1) - 1)
    def _():
        y = x_ref[...] + acc_ref[...]
        if final_norm:
            ms = jnp.mean(y * y, axis=-1, keepdims=True)
            y = y * lax.rsqrt(ms + EPS) * gf_ref[...]
        o_ref[...] = y


def _mlp(x, g, w_up, w_down, g_final, final_norm, tm=512, tf=512):
    m, d = x.shape
    dff = w_up.shape[1]
    return pl.pallas_call(
        functools.partial(_mlp_kernel, final_norm=final_norm),
        grid=(m // tm, dff // tf),
        in_specs=[
            pl.BlockSpec((tm, d), lambda i, f: (i, 0)),
            pl.BlockSpec((1, d), lambda i, f: (0, 0)),
            pl.BlockSpec((d, tf), lambda i, f: (0, f)),
            pl.BlockSpec((tf, d), lambda i, f: (f, 0)),
            pl.BlockSpec((1, d), lambda i, f: (0, 0)),
        ],
        out_specs=pl.BlockSpec((tm, d), lambda i, f: (i, 0)),
        out_shape=jax.ShapeDtypeStruct((m, d), F32),
        scratch_shapes=[pltpu.VMEM((tm, d), BF16), pltpu.VMEM((tm, d), F32)],
        compiler_params=_params("parallel", "arbitrary"),
    )(x, g.reshape(1, d), w_up, w_down, g_final.reshape(1, d))


def _shr(x, pow2):
    assert pow2 & (pow2 - 1) == 0
    return lax.shift_right_logical(x, pow2.bit_length() - 1)


def _masked_softmax(s, mask):
    sm = jnp.where(mask, s, NEG)
    e = jnp.exp(sm - jnp.max(sm, axis=-1, keepdims=True))
    return e / jnp.sum(e, axis=-1, keepdims=True)


def _select_blocks(imp, qpos, n_loop):
    r, nb = imp.shape
    bidx = lax.broadcasted_iota(jnp.int32, (r, nb), 1)
    cur = _shr(qpos, BLK)
    forced = (bidx == 0) | (bidx == cur) | (bidx == cur - 1)
    cand = bidx * BLK <= qpos
    score = jnp.where(cand, jnp.where(forced, FORCED, imp), NEG)
    rank = jnp.zeros((r, nb), F32)
    for j in range(n_loop):
        c = score[:, j:j + 1]
        beats = (c > score) | ((c == score) & (bidx > j))
        rank = rank + jnp.where(beats, 1.0, 0.0)
    return jnp.where(cand & (rank < TOP_N), 1.0, 0.0)


def _select_blocks_t(imp_t, qpos_t):
    nb, r = imp_t.shape
    bidx = lax.broadcasted_iota(jnp.int32, (nb, r), 0)
    cur = _shr(qpos_t, BLK)
    forced = (bidx == 0) | (bidx == cur) | (bidx == cur - 1)
    cand = bidx * BLK <= qpos_t
    score = jnp.where(cand, jnp.where(forced, FORCED, imp_t), NEG)
    n_chunks = nb // SUBLANES
    chunks = [score[SUBLANES * c:SUBLANES * (c + 1)] for c in range(n_chunks)]
    ranks = [jnp.zeros((SUBLANES, r), F32) for _ in range(n_chunks)]
    sub = lax.broadcasted_iota(jnp.int32, (SUBLANES, r), 0)
    for j in range(nb):
        cj = score[j:j + 1, :]
        for c in range(n_chunks):
            lo = SUBLANES * c
            ge = jnp.where(cj >= chunks[c], 1.0, 0.0)
            gt = jnp.where(cj > chunks[c], 1.0, 0.0)
            if lo > j:
                term = ge
            elif lo + SUBLANES - 1 < j:
                term = gt
            else:
                term = jnp.where(sub > j - lo, ge, gt)
            ranks[c] = ranks[c] + term
    rank = jnp.concatenate(ranks, axis=0)
    return jnp.where(cand & (rank < TOP_N), 1.0, 0.0)


def _block_expand(n_blocks, k0, n_keys):
    b = lax.broadcasted_iota(jnp.int32, (n_blocks, n_keys), 0)
    k = lax.broadcasted_iota(jnp.int32, (n_blocks, n_keys), 1) + k0
    return jnp.where(_shr(k, BLK) == b, 1.0, 0.0).astype(BF16)


def _softmax_keys(scores, mask):
    sm = jnp.where(mask, scores, NEG)
    e = jnp.exp(sm - jnp.max(sm, axis=0, keepdims=True))
    return e / jnp.sum(e, axis=0, keepdims=True)


def _nsa_prompt_kernel(q_ref, kc_ref, vc_ref, ks_ref, vs_ref, kw_ref, vw_ref, g_ref,
                       o_ref, *, seq, heads):
    i = pl.program_id(2)
    rows = GROUP * Q_TILE
    hw = GROUP * HEAD_DIM
    nb = seq // BLK
    qpos = i * Q_TILE + (lax.broadcasted_iota(jnp.int32, (1, rows), 1) & (Q_TILE - 1))
    bcol = lax.broadcasted_iota(jnp.int32, (nb, 1), 0)
    bidx = lax.broadcasted_iota(jnp.int32, (1, nb), 1)
    cmask = bcol * BLK + (BLK - 1) <= qpos
    own = _shr(bidx, Q_TILE // BLK) == i

    def prepare(h):
        qt = q_ref[:, h * hw:(h + 1) * hw]
        q = jnp.concatenate([qt[:, g * HEAD_DIM:(g + 1) * HEAD_DIM] for g in range(GROUP)], axis=0)
        s_c = lax.dot_general(kc_ref[0, h], q, _NT, preferred_element_type=F32)
        p_c = jnp.where(cmask, _softmax_keys(s_c, cmask), 0.0)
        o_cmp = jnp.dot(vc_ref[0, h], p_c.astype(BF16), preferred_element_type=F32)
        imp = p_c[:, 0:Q_TILE]
        for g in range(1, GROUP):
            imp = imp + p_c[:, g * Q_TILE:(g + 1) * Q_TILE]
        sel = _select_blocks_t(imp, qpos[:, 0:Q_TILE]).T
        block_bias = jnp.where((sel > 0.5) & jnp.logical_not(own), 0.0, NEG).astype(BF16)
        q_aug = jnp.concatenate([q, jnp.concatenate([block_bias] * GROUP, axis=0)], axis=1)
        return q, q_aug, o_cmp

    def online_step(carry, scores, v_t):
        m, l, acc = carry
        m_new = jnp.maximum(m, jnp.max(scores, axis=0, keepdims=True))
        alpha = jnp.exp(m - m_new)
        pj = jnp.exp(scores - m_new)
        l = alpha * l + jnp.sum(pj, axis=0, keepdims=True)
        acc = alpha * acc + jnp.dot(v_t, pj.astype(BF16), preferred_element_type=F32)
        return m_new, l, acc

    prepared = [prepare(h) for h in range(heads)]

    def body(j, carries):
        k0 = pl.multiple_of(j * KEY_TILE, KEY_TILE)
        scores = [lax.dot_general(ks_ref[0, h, pl.ds(k0, KEY_TILE), :], prepared[h][1], _NT,
                                  preferred_element_type=F32) for h in range(heads)]
        return tuple(online_step(carries[h], scores[h], vs_ref[0, h, :, pl.ds(k0, KEY_TILE)])
                     for h in range(heads))

    n_tiles = lax.div(i * Q_TILE + KEY_TILE - 1, KEY_TILE)
    init = tuple((jnp.full((1, rows), NEG, F32), jnp.zeros((1, rows), F32),
                  jnp.zeros((HEAD_DIM, rows), F32)) for _ in range(heads))
    carries = lax.fori_loop(0, n_tiles, body, init)

    d0 = pl.multiple_of(i * Q_TILE, Q_TILE)
    causal = d0 + lax.broadcasted_iota(jnp.int32, (Q_TILE, 1), 0) <= qpos
    span = WINDOW + Q_TILE
    start = pl.multiple_of(jnp.maximum(i * Q_TILE - WINDOW, 0), Q_TILE)
    wpos = start + lax.broadcasted_iota(jnp.int32, (span, 1), 0)
    wmask = (wpos <= qpos) & (wpos >= qpos - WINDOW)
    zero_bias = jnp.zeros((rows, nb), BF16)
    sd = [lax.dot_general(ks_ref[0, h, pl.ds(d0, Q_TILE), :],
                          jnp.concatenate([prepared[h][0], zero_bias], axis=1), _NT,
                          preferred_element_type=F32) for h in range(heads)]
    sw = [lax.dot_general(kw_ref[0, h, pl.ds(start, span), :], prepared[h][0], _NT,
                          preferred_element_type=F32) for h in range(heads)]
    for h in range(heads):
        o_cmp = prepared[h][2]
        _, l, acc = online_step(carries[h], jnp.where(causal, sd[h], NEG),
                                vs_ref[0, h, :, pl.ds(d0, Q_TILE)])
        o_sel = acc / l
        vw_t = vw_ref[0, h, :, pl.ds(start, span)]
        o_win = jnp.dot(vw_t, _softmax_keys(sw[h], wmask).astype(BF16), preferred_element_type=F32)

        g_t = g_ref[:, h * LANES:(h + 1) * LANES].T
        for g in range(GROUP):
            sl = slice(g * Q_TILE, (g + 1) * Q_TILE)
            o_g = (g_t[g:g + 1] * o_cmp[:, sl] + g_t[GROUP + g:GROUP + g + 1] * o_sel[:, sl]
                   + g_t[2 * GROUP + g:2 * GROUP + g + 1] * o_win[:, sl])
            c0 = h * hw + g * HEAD_DIM
            o_ref[:, c0:c0 + HEAD_DIM] = o_g.T.astype(o_ref.dtype)


def _nsa_prompt(q, kc, vc, ks, vs, kw, vw, gates, n, seq, heads=4):
    nq = seq // Q_TILE
    nb = seq // BLK
    onehot = (jnp.arange(seq)[:, None] // BLK == jnp.arange(nb)[None, :]).astype(BF16)
    ks = jnp.concatenate([ks, jnp.broadcast_to(onehot, ks.shape[:2] + onehot.shape)], axis=-1)
    row = lambda b, h, i: (b * nq + i, h)
    head = lambda b, h, i: (b, h, 0, 0)
    k_spec = pl.BlockSpec((1, heads, seq, HEAD_DIM), head)
    v_spec = pl.BlockSpec((1, heads, HEAD_DIM, seq), head)
    return pl.pallas_call(
        functools.partial(_nsa_prompt_kernel, seq=seq, heads=heads),
        grid=(n, N_KV // heads, nq),
        in_specs=[
            pl.BlockSpec((Q_TILE, heads * GROUP * HEAD_DIM), row),
            pl.BlockSpec((1, heads, nb, HEAD_DIM), head), pl.BlockSpec((1, heads, HEAD_DIM, nb), head),
            pl.BlockSpec((1, heads, seq, HEAD_DIM + nb), head), v_spec, k_spec, v_spec,
            pl.BlockSpec((Q_TILE, heads * LANES), row),
        ],
        out_specs=pl.BlockSpec((Q_TILE, heads * GROUP * HEAD_DIM), row),
        out_shape=jax.ShapeDtypeStruct((n * seq, N_KV * GROUP * HEAD_DIM), BF16),
        compiler_params=_params("parallel", "parallel", "arbitrary"),
    )(q, kc, vc, ks, vs, kw, vw, gates)


def _nsa_sample_kernel(pt_ref, q_ref, g_ref, *refs, n_pages, page, dec, wb):
    del pt_ref
    cmp_pages = refs[:n_pages]
    cmp_new = refs[n_pages]
    sel_pages = refs[n_pages + 1:2 * n_pages + 1]
    sel_new = refs[2 * n_pages + 1]
    win_ref, win_new, o_ref, wo_ref, kc_ref, k_ref, v_ref, kw_ref, vw_ref = refs[2 * n_pages + 2:]
    past = n_pages * page
    rows = N_KV * GROUP * dec
    q = q_ref[0]
    ridx = lax.broadcasted_iota(jnp.int32, (rows, 1), 0)
    qpos = past + (ridx & (dec - 1))

    n_keys = past + page
    for p_i in range(n_pages + 1):
        pg = sel_pages[p_i][0] if p_i < n_pages else sel_new[0]
        k_ref[:, p_i * page:(p_i + 1) * page] = pg[:KV_W].astype(BF16)
        v_ref[:, p_i * page:(p_i + 1) * page] = pg[KV_W:].astype(BF16)
    wn = win_ref[0]
    nw = win_new[0]
    kw_ref[:, 0:wb] = wn[:KV_W].astype(BF16)
    vw_ref[:, 0:wb] = wn[KV_W:].astype(BF16)
    kw_ref[:, wb:wb + page] = nw[:KV_W].astype(BF16)
    vw_ref[:, wb:wb + page] = nw[KV_W:].astype(BF16)
    ss = jnp.dot(q, k_ref[...], preferred_element_type=F32)
    sw = jnp.dot(q, kw_ref[...], preferred_element_type=F32)

    bpp = page // BLK
    kc_ref[...] = jnp.zeros_like(kc_ref)
    for p_i in range(n_pages):
        kc_ref[bpp * p_i:bpp * (p_i + 1), :] = cmp_pages[p_i][0]
    kc_ref[bpp * n_pages:bpp * n_pages + 1, :] = cmp_new[0, 0:1, :]
    nbp = kc_ref.shape[0]
    s = lax.dot_general(q, kc_ref[:, :KV_W].astype(BF16), _NT, preferred_element_type=F32)
    bidx = lax.broadcasted_iota(jnp.int32, (1, nbp), 1)
    cmask = bidx * BLK + (BLK - 1) <= qpos
    p = jnp.where(cmask, _masked_softmax(s, cmask), 0.0)
    o_cmp = jnp.dot(p.astype(BF16), kc_ref[:, KV_W:].astype(BF16), preferred_element_type=F32)

    imps = []
    for h in range(N_KV):
        base = h * GROUP * dec
        acc = p[base:base + dec]
        for g in range(1, GROUP):
            acc = acc + p[base + g * dec:base + (g + 1) * dec]
        imps.append(acc)
    imp = jnp.concatenate(imps, axis=0)
    qpos_ht = past + (lax.broadcasted_iota(jnp.int32, (N_KV * dec, 1), 0) & (dec - 1))
    sel = _select_blocks(imp, qpos_ht, bpp * n_pages + 1)
    sel_rows = jnp.concatenate(
        [sel[h * dec:(h + 1) * dec] for h in range(N_KV) for _ in range(GROUP)], axis=0)

    wpos = past - wb + lax.broadcasted_iota(jnp.int32, (1, wb + page), 1)
    pw = _masked_softmax(sw, (wpos <= qpos) & (wpos >= qpos - WINDOW))
    o_win = lax.dot_general(pw.astype(BF16), vw_ref[...], _NT, preferred_element_type=F32)

    key_sel = jnp.dot(sel_rows.astype(BF16), _block_expand(nbp, 0, n_keys),
                      preferred_element_type=F32)
    kpos = lax.broadcasted_iota(jnp.int32, (1, n_keys), 1)
    ps = _masked_softmax(ss, (key_sel > 0.5) & (kpos <= qpos))
    o_sel = lax.dot_general(ps.astype(BF16), v_ref[...], _NT, preferred_element_type=F32)

    gt = g_ref[0]
    o = gt[:, 0:1] * o_cmp + gt[:, 1:2] * o_sel + gt[:, 2:3] * o_win
    col = lax.broadcasted_iota(jnp.int32, (1, KV_W), 1)
    o = jnp.where(_shr(ridx, GROUP * dec) == _shr(col, HEAD_DIM), o, 0.0)
    hr = GROUP * dec
    out = o[0:hr]
    for h in range(1, N_KV):
        out = out + o[h * hr:(h + 1) * hr]
    o_ref[0] = out

    shifted = pltpu.roll(wn, wb - dec, 1)
    tail = pltpu.roll(nw, page - dec, 1)
    lane = lax.broadcasted_iota(jnp.int32, (1, page), 1)
    wo_ref[0, :, 0:wb - page] = shifted[:, 0:wb - page]
    wo_ref[0, :, wb - page:wb] = jnp.where(lane >= page - dec, tail, shifted[:, wb - page:wb])


def _nsa_sample(page_table, layer, qbd, gates, cmp_phys, cmp_new, sel_cache_t, sel_new,
                win_cache_t, win_new):
    nseq, n_pages = page_table.shape
    n_phys = cmp_phys.shape[0]
    page = sel_cache_t.shape[2]
    rows = qbd.shape[1]
    dec = rows // (N_KV * GROUP)
    wb = win_cache_t.shape[2]
    bpp = page // BLK
    nbp = LANES
    assert bpp * n_pages + 1 <= nbp and wb == WINDOW and page == LANES and dec <= SUBLANES

    def page_map(p_i, base):
        return lambda b, pt: (base + pt[b * n_pages + p_i], 0, 0)

    seq_map = lambda b, pt: (b, 0, 0)
    in_specs = [
        pl.BlockSpec((1, rows, KV_W), seq_map),
        pl.BlockSpec((1, rows, 8), seq_map),
    ]
    in_specs += [pl.BlockSpec((1, bpp, 2 * KV_W), page_map(p_i, 0)) for p_i in range(n_pages)]
    in_specs += [pl.BlockSpec((1, 8, 2 * KV_W), seq_map)]
    in_specs += [pl.BlockSpec((1, 2 * KV_W, page), page_map(p_i, layer * n_phys))
                 for p_i in range(n_pages)]
    in_specs += [
        pl.BlockSpec((1, 2 * KV_W, page), seq_map),
        pl.BlockSpec((1, 2 * KV_W, wb), lambda b, pt: (layer * nseq + b, 0, 0)),
        pl.BlockSpec((1, 2 * KV_W, page), seq_map),
    ]
    n_keys = (n_pages + 1) * page
    grid_spec = pltpu.PrefetchScalarGridSpec(
        num_scalar_prefetch=1,
        grid=(nseq,),
        in_specs=in_specs,
        out_specs=[
            pl.BlockSpec((1, GROUP * dec, KV_W), seq_map),
            pl.BlockSpec((1, 2 * KV_W, wb), seq_map),
        ],
        scratch_shapes=[
            pltpu.VMEM((nbp, 2 * KV_W), F32),
            pltpu.VMEM((KV_W, n_keys), BF16),
            pltpu.VMEM((KV_W, n_keys), BF16),
            pltpu.VMEM((KV_W, wb + page), BF16),
            pltpu.VMEM((KV_W, wb + page), BF16),
        ],
    )
    return pl.pallas_call(
        functools.partial(_nsa_sample_kernel, n_pages=n_pages, page=page, dec=dec, wb=wb),
        grid_spec=grid_spec,
        out_shape=[
            jax.ShapeDtypeStruct((nseq, GROUP * dec, KV_W), F32),
            jax.ShapeDtypeStruct((nseq, 2 * KV_W, wb), F32),
        ],
        compiler_params=_params("arbitrary"),
    )(page_table.reshape(-1), qbd, gates, *([cmp_phys] * n_pages), cmp_new,
      *([sel_cache_t] * n_pages), sel_new, win_cache_t, win_new)


def _heads_major(a, n, seq):
    return a.reshape(n, seq, N_KV, HEAD_DIM).transpose(0, 2, 1, 3).astype(BF16)


def _heads_major_t(a, n, seq):
    return a.reshape(n, seq, N_KV, HEAD_DIM).transpose(0, 2, 3, 1).astype(BF16)


def _pad_rows(a, rows):
    return jnp.pad(a, ((0, 0), (0, rows - a.shape[1]), (0, 0)))


def _channel_major(cache):
    depth, a, rows = cache.shape[:3]
    return cache.transpose(0, 1, 3, 4, 5, 2).reshape(depth * a, 2 * KV_W, rows)


def kernel(x_prompt, x_sample, cache_cmp_kv, cache_sel_kv, cache_win_kv, state_pool, page_table, g_mix, w_in, pe_ck, w_ck1, w_ck2, pe_cv, w_cv1, w_cv2, w_pool_grp, pool_scale, w_br_nsa, w_br_pool, w_o, g_mlp, w_up, w_down, g_final):
    n, seq, d = x_prompt.shape
    nseq, dec, _ = x_sample.shape
    depth = w_in.shape[0]
    n_phys, page = cache_cmp_kv.shape[1], cache_cmp_kv.shape[2]
    wb = cache_win_kv.shape[2]
    mp, ms = n * seq, nseq * dec
    nsa_w = N_KV * GROUP * HEAD_DIM
    kv_cols = 2 * KV_W
    n_gate = 3 * N_KV * GROUP
    pool_w = state_pool.shape[-1]
    pool_state = state_pool.shape[2]
    c_q, c_kv, c_ng = nsa_w, nsa_w + 3 * kv_cols, nsa_w + 3 * kv_cols + n_gate
    c_u = c_ng + pool_w

    x = jnp.concatenate([x_prompt.reshape(mp, d), x_sample.reshape(ms, d)], axis=0)
    kv_shape = (2, N_KV, HEAD_DIM)
    outs = [[] for _ in range(8)]
    cmp_cache_t = _channel_major(cache_cmp_kv).reshape(depth * n_phys * kv_cols, page)
    sel_cache_t = _channel_major(cache_sel_kv)
    win_cache_t = _channel_major(cache_win_kv)

    for l in range(depth):
        w = w_in[l]
        w_q = (w[:, :c_q] * HEAD_DIM ** -0.5).astype(BF16)
        w_kv = w[:, c_q:c_kv].astype(BF16)
        w_ng = w[:, c_kv:c_ng].reshape(d, 3, N_KV, GROUP).transpose(0, 2, 1, 3).reshape(d, N_KV, 3 * GROUP)
        w_ng = jnp.pad(w_ng, ((0, 0), (0, 0), (0, LANES - 3 * GROUP))).reshape(d, N_KV * LANES).astype(BF16)
        w_u = w[:, c_ng:c_u].astype(BF16)
        w_mg = w[:, c_u:].astype(BF16)

        zkv = _norm_matmul(x, g_mix[l], w_kv, F32)
        q = _norm_matmul(x, g_mix[l], w_q, BF16)
        u = _norm_matmul(x, g_mix[l], w_u, F32)
        ng = _norm_matmul(x, g_mix[l], w_ng, F32, sigmoid=True)
        mg = _norm_matmul(x, g_mix[l], w_mg, BF16, sigmoid=True)

        cmp_w = (pe_ck[l], w_ck1[l], w_ck2[l], pe_cv[l], w_cv1[l], w_cv2[l])
        cw = _compress_weights(*cmp_w)

        cmp_p = _compress(zkv, mp // BLK, *cw)
        kc_p = _heads_major(cmp_p[:, :KV_W], n, seq // BLK)
        vc_p = _heads_major_t(cmp_p[:, KV_W:], n, seq // BLK)
        zp = zkv[:mp]
        col = lambda c: zp[:, c * KV_W:(c + 1) * KV_W]
        o_nsa_p = _nsa_prompt(q, kc_p, vc_p, _heads_major(col(2), n, seq), _heads_major_t(col(3), n, seq),
                              _heads_major(col(4), n, seq), _heads_major_t(col(5), n, seq), ng, n, seq)

        zs = zkv[mp:].reshape(nseq, dec, 3 * kv_cols)
        new_c = _pad_rows(zs[:, :, :kv_cols], page)
        lanes_t = lambda a: jnp.pad(a.transpose(0, 2, 1), ((0, 0), (0, 0), (0, page - dec)))
        new_s = lanes_t(zs[:, :, kv_cols:2 * kv_cols])
        new_w = lanes_t(zs[:, :, 2 * kv_cols:])
        bpp = page // BLK
        cmp_phys = _compress_pages(cmp_cache_t, l, n_phys, *_compress_page_weights(*cmp_w))
        cmp_phys = cmp_phys.reshape(n_phys, 2, N_KV, bpp, HEAD_DIM).transpose(0, 3, 1, 2, 4)
        cmp_phys = cmp_phys.reshape(n_phys, bpp, kv_cols)
        cmp_new = _compress(new_c.reshape(nseq * page, kv_cols), nseq * bpp, *cw)
        cmp_new = _pad_rows(cmp_new.reshape(nseq, bpp, kv_cols)[:, :1], 8)

        qs = q[mp:].reshape(nseq, dec, N_KV, GROUP, HEAD_DIM)
        eye = jnp.eye(N_KV, dtype=BF16)
        qbd = jnp.einsum('bthgd,hk->bhgtkd', qs, eye).reshape(nseq, N_KV * GROUP * dec, KV_W)
        gs = ng[mp:].reshape(nseq, dec, N_KV, LANES)[..., :3 * GROUP].reshape(nseq, dec, N_KV, 3, GROUP)
        gs = gs.transpose(0, 2, 4, 1, 3).reshape(nseq, N_KV * GROUP * dec, 3)
        gs = jnp.pad(gs, ((0, 0), (0, 0), (0, 5)))
        o_s, win_t = _nsa_sample(page_table, l, qbd, gs, cmp_phys, cmp_new, sel_cache_t, new_s,
                                 win_cache_t, new_w)
        o_nsa_s = o_s.reshape(nseq, GROUP, dec, N_KV, HEAD_DIM).transpose(0, 2, 3, 1, 4)
        o_nsa = jnp.concatenate([o_nsa_p, o_nsa_s.reshape(ms, nsa_w).astype(BF16)], axis=0)

        w_grp = w_pool_grp[l].astype(BF16)
        u_p, u_s = u[:mp], u[mp:].reshape(nseq, dec, pool_w)
        o_pool_p = _pool_mix(u_p, w_grp, pool_scale[l], 512, seq // 512)
        hist = jnp.pad(state_pool[l], ((0, 0), (POOL_HALO - pool_state, 0), (0, 0)))
        xx_s = jnp.concatenate([hist, u_s], axis=1)
        rows_s = POOL_HALO + dec
        o_pool_s = _pool_mix(xx_s.reshape(nseq * rows_s, pool_w), w_grp, pool_scale[l],
                             16 * rows_s, None)
        o_pool_s = o_pool_s.reshape(nseq, rows_s, pool_w)[:, POOL_HALO:].reshape(ms, pool_w)
        o_pool = jnp.concatenate([o_pool_p, o_pool_s], axis=0)

        m = _gated_branches(o_nsa, o_pool, w_br_nsa[l].astype(BF16), w_br_pool[l].astype(BF16), mg)
        x1 = _resid_matmul(x, m, w_o[l].astype(BF16))
        x = _mlp(x1, g_mlp[l], w_up[l].astype(BF16), w_down[l].astype(BF16), g_final,
                 final_norm=(l == depth - 1))

        zp6 = zp.reshape(n, seq, 3, *kv_shape)
        outs[0].append(zp6[:, :, 0])
        outs[1].append(zp6[:, :, 1])
        outs[2].append(zp6[:, -min(WINDOW, seq):, 2])
        outs[3].append(u_p.reshape(n, seq, pool_w)[:, -pool_state:])
        zs6 = zs.reshape(nseq, dec, 3, *kv_shape)
        outs[4].append(zs6[:, :, 0])
        outs[5].append(zs6[:, :, 1])
        outs[6].append(win_t.reshape(nseq, *kv_shape, wb).transpose(0, 4, 1, 2, 3))
        outs[7].append(xx_s[:, -pool_state:])

    y_prompt = x[:mp].reshape(n, seq, d)
    y_sample = x[mp:].reshape(nseq, dec, d)
    return (y_prompt, y_sample) + tuple(jnp.stack(o) for o in outs)
```

```python
import functools

import jax
import jax.numpy as jnp
from jax import lax
from jax.experimental import pallas as pl
from jax.experimental.pallas import tpu as pltpu

F32 = jnp.float32
BF16 = jnp.bfloat16

N_KV = 4
GROUP = 4
HEAD_DIM = 64
KV_W = N_KV * HEAD_DIM
BLK = 64
TOP_N = 16
WINDOW = 512
Q_TILE = 128
KEY_TILE = 512
POOL_WINDOWS = (2, 4, 8, 16)
POOL_HALO = 16
EPS = 1e-6
NEG = -1e30
FORCED = 1e4
LANES = 128
SUBLANES = 8
VMEM_LIMIT = 56 * 1024 * 1024

_NT = (((1,), (1,)), ((), ()))


def _params(*sem):
    return pltpu.CompilerParams(dimension_semantics=sem, vmem_limit_bytes=VMEM_LIMIT)


def _norm_matmul_kernel(x_ref, g_ref, w_ref, o_ref, xn_ref, *, sigmoid):
    @pl.when(pl.program_id(1) == 0)
    def _():
        x = x_ref[...]
        ms = jnp.mean(x * x, axis=-1, keepdims=True)
        xn_ref[...] = (x * lax.rsqrt(ms + EPS) * g_ref[...]).astype(BF16)

    z = lax.dot_general(xn_ref[...], w_ref[0], _NT, preferred_element_type=F32)
    if sigmoid:
        z = jax.nn.sigmoid(z)
    o_ref[...] = z.astype(o_ref.dtype)


def _norm_matmul(x, g, w_t, layer, out_dtype, sigmoid=False, tm=1024, tn=512):
    m, d = x.shape
    n = w_t.shape[1]
    tm, tn = min(tm, m), min(tn, n)
    return pl.pallas_call(
        functools.partial(_norm_matmul_kernel, sigmoid=sigmoid),
        grid=(m // tm, n // tn),
        in_specs=[
            pl.BlockSpec((tm, d), lambda i, j: (i, 0)),
            pl.BlockSpec((1, d), lambda i, j: (0, 0)),
            pl.BlockSpec((1, tn, d), lambda i, j: (layer, j, 0)),
        ],
        out_specs=pl.BlockSpec((tm, tn), lambda i, j: (i, j)),
        out_shape=jax.ShapeDtypeStruct((m, n), out_dtype),
        scratch_shapes=[pltpu.VMEM((tm, d), BF16)],
        compiler_params=_params("parallel", "arbitrary"),
    )(x, g.reshape(1, d), w_t)


def _compress_kernel(x_ref, pe_ref, w1_ref, w2_ref, o_ref, *, tb):
    acc = jnp.zeros((tb, 2 * LANES), F32)
    for r in range(BLK):
        xr = x_ref[pl.ds(r, tb, stride=BLK), :] + pe_ref[0, r:r + 1, :]
        acc = acc + jnp.dot(xr.astype(BF16), w1_ref[0, r], preferred_element_type=F32)
    hid = jax.nn.gelu(acc)
    o_ref[...] = jnp.dot(hid.astype(BF16), w2_ref[0], preferred_element_type=F32)


def _compress(rows, n_blocks, pe2, w1bd, w2bd, tb=128):
    return pl.pallas_call(
        functools.partial(_compress_kernel, tb=tb),
        grid=(n_blocks // tb, 4),
        in_specs=[
            pl.BlockSpec((tb * BLK, LANES), lambda i, c: (i, c)),
            pl.BlockSpec((1, BLK, LANES), lambda i, c: (c // 2, 0, 0)),
            pl.BlockSpec((1, BLK, LANES, 2 * LANES), lambda i, c: (c // 2, 0, 0, 0)),
            pl.BlockSpec((1, 2 * LANES, LANES), lambda i, c: (c // 2, 0, 0)),
        ],
        out_specs=pl.BlockSpec((tb, LANES), lambda i, c: (i, c)),
        out_shape=jax.ShapeDtypeStruct((n_blocks, 4 * LANES), F32),
        compiler_params=_params("parallel", "arbitrary"),
    )(rows, pe2, w1bd, w2bd)


def _compress_weights(pe_k, w1_k, w2_k, pe_v, w1_v, w2_v):
    def one(pe, w1, w2):
        hid = w1.shape[1]
        w1r = w1.reshape(BLK, HEAD_DIM, hid)
        z = jnp.zeros_like(w1r)
        w1bd = jnp.concatenate(
            [jnp.concatenate([w1r, z], axis=2), jnp.concatenate([z, w1r], axis=2)], axis=1)
        z2 = jnp.zeros_like(w2)
        w2bd = jnp.concatenate(
            [jnp.concatenate([w2, z2], axis=1), jnp.concatenate([z2, w2], axis=1)], axis=0)
        return jnp.concatenate([pe, pe], axis=1), w1bd.astype(BF16), w2bd.astype(BF16)

    k, v = one(pe_k, w1_k, w2_k), one(pe_v, w1_v, w2_v)
    return tuple(jnp.stack([a, b]) for a, b in zip(k, v))


def _compress_pages_kernel(x_ref, pe_ref, w1_ref, w2_ref, o_ref, *, rows):
    ridx = lax.broadcasted_iota(jnp.int32, (rows, 1), 0)
    is_v = (_shr(ridx, N_KV) & 1) == 1
    acc = jnp.zeros((rows, 2 * LANES), F32)
    for d in range(HEAD_DIM):
        a = x_ref[pl.ds(d, rows, stride=HEAD_DIM), :]
        ak = jnp.where(is_v, 0.0, a + pe_ref[0, d:d + 1, :])
        av = jnp.where(is_v, a + pe_ref[1, d:d + 1, :], 0.0)
        lhs = jnp.concatenate([ak, av], axis=1).astype(BF16)
        acc = acc + jnp.dot(lhs, w1_ref[d], preferred_element_type=F32)
    hid = jax.nn.gelu(acc)
    lhs = jnp.concatenate([jnp.where(is_v, 0.0, hid), jnp.where(is_v, hid, 0.0)], axis=1)
    o_ref[...] = jnp.dot(lhs.astype(BF16), w2_ref[...], preferred_element_type=F32)


def _compress_pages(cache_t, layer, n_phys, pe_t, w1p, w2p, pages=16):
    rows = 2 * N_KV * pages
    steps = n_phys // pages
    return pl.pallas_call(
        functools.partial(_compress_pages_kernel, rows=rows),
        grid=(steps,),
        in_specs=[
            pl.BlockSpec((rows * HEAD_DIM, LANES), lambda i: (layer * steps + i, 0)),
            pl.BlockSpec(pe_t.shape, lambda i: (0, 0, 0)),
            pl.BlockSpec(w1p.shape, lambda i: (0, 0, 0)),
            pl.BlockSpec(w2p.shape, lambda i: (0, 0)),
        ],
        out_specs=pl.BlockSpec((rows, LANES), lambda i: (i, 0)),
        out_shape=jax.ShapeDtypeStruct((n_phys * 2 * N_KV, LANES), F32),
        compiler_params=_params("parallel"),
    )(cache_t, pe_t, w1p, w2p)


def _compress_page_weights(pe_k, w1_k, w2_k, pe_v, w1_v, w2_v):
    eye = jnp.eye(2, dtype=F32)
    hid = w1_k.shape[1]

    def w1p(w1):
        w1d = w1.reshape(BLK, HEAD_DIM, hid).transpose(1, 0, 2)
        return jnp.einsum('drj,bc->dbrcj', w1d, eye)

    def w2p(w2):
        return jnp.einsum('je,bc->bjce', w2, eye)

    w1 = jnp.stack([w1p(w1_k), w1p(w1_v)], axis=1).reshape(HEAD_DIM, 4 * BLK, 2 * hid)
    w2 = jnp.stack([w2p(w2_k), w2p(w2_v)]).reshape(4 * hid, 2 * HEAD_DIM)
    pe_t = jnp.stack([jnp.tile(pe_k.T, (1, 2)), jnp.tile(pe_v.T, (1, 2))])
    return pe_t, w1.astype(BF16), w2.astype(BF16)


def _pool_kernel(u_ref, halo_ref, w_ref, s_ref, o_ref, *, tt, tiles_per_seq):
    i = pl.program_id(0)
    u = u_ref[...]
    halo = halo_ref[...]
    if tiles_per_seq is not None:
        halo = jnp.where(i % tiles_per_seq == 0, 0.0, halo)
        pos = (i % tiles_per_seq) * tt + lax.broadcasted_iota(jnp.int32, (tt, 1), 0)
    a = jnp.concatenate([halo, u], axis=0)
    gw = u.shape[1] // len(POOL_WINDOWS)
    for gi, w in enumerate(POOL_WINDOWS):
        b = a[:, gi * gw:(gi + 1) * gw]
        width = 1
        while width < w:
            b = b[:b.shape[0] - width] + b[width:]
            width *= 2
        lo = POOL_HALO + 1 - w
        win = b[lo:lo + tt]
        if tiles_per_seq is None:
            mean = win / float(w)
        else:
            mean = win / jnp.minimum(pos + 1, w).astype(F32)
        d = mean - u[:, gi * gw:(gi + 1) * gw]
        y = jnp.dot(d.astype(BF16), w_ref[gi], preferred_element_type=F32)
        o_ref[:, gi * gw:(gi + 1) * gw] = (y * s_ref[:, gi * gw:(gi + 1) * gw]).astype(o_ref.dtype)


def _pool_mix(u, w_grp, scale, tt, tiles_per_seq):
    m, width = u.shape
    hb = tt // POOL_HALO
    return pl.pallas_call(
        functools.partial(_pool_kernel, tt=tt, tiles_per_seq=tiles_per_seq),
        grid=(m // tt,),
        in_specs=[
            pl.BlockSpec((tt, width), lambda i: (i, 0)),
            pl.BlockSpec((POOL_HALO, width), lambda i: (jnp.maximum(i * hb - 1, 0), 0)),
            pl.BlockSpec(w_grp.shape, lambda i: (0, 0, 0)),
            pl.BlockSpec((1, width), lambda i: (0, 0)),
        ],
        out_specs=pl.BlockSpec((tt, width), lambda i: (i, 0)),
        out_shape=jax.ShapeDtypeStruct((m, width), BF16),
        compiler_params=_params("parallel"),
    )(u, u, w_grp, scale.reshape(1, width))


def _gated_branch_kernel(a_ref, b_ref, wa_ref, wb_ref, ga_ref, gb_ref, o_ref):
    a = jnp.dot(a_ref[...], wa_ref[0], preferred_element_type=F32)
    b = jnp.dot(b_ref[...], wb_ref[0], preferred_element_type=F32)
    o_ref[...] = (ga_ref[...].astype(F32) * a + gb_ref[...].astype(F32) * b).astype(o_ref.dtype)


def _gated_branches(a, b, wa, wb, gates, layer, tm=1024, tn=512):
    m, k = a.shape
    n = wa.shape[2]
    nj = n // tn
    return pl.pallas_call(
        _gated_branch_kernel,
        grid=(m // tm, nj),
        in_specs=[
            pl.BlockSpec((tm, k), lambda i, j: (i, 0)),
            pl.BlockSpec((tm, k), lambda i, j: (i, 0)),
            pl.BlockSpec((1, k, tn), lambda i, j: (layer, 0, j)),
            pl.BlockSpec((1, k, tn), lambda i, j: (layer, 0, j)),
            pl.BlockSpec((tm, tn), lambda i, j: (i, j)),
            pl.BlockSpec((tm, tn), lambda i, j: (i, j + nj)),
        ],
        out_specs=pl.BlockSpec((tm, tn), lambda i, j: (i, j)),
        out_shape=jax.ShapeDtypeStruct((m, n), BF16),
        compiler_params=_params("parallel", "arbitrary"),
    )(a, b, wa, wb, gates, gates)


def _resid_matmul_kernel(x_ref, a_ref, w_ref, o_ref):
    o_ref[...] = x_ref[...] + jnp.dot(a_ref[...], w_ref[0], preferred_element_type=F32)


def _resid_matmul(x, a, w, layer, tm=1024, tn=512):
    m, k = a.shape
    n = w.shape[2]
    return pl.pallas_call(
        _resid_matmul_kernel,
        grid=(m // tm, n // tn),
        in_specs=[
            pl.BlockSpec((tm, tn), lambda i, j: (i, j)),
            pl.BlockSpec((tm, k), lambda i, j: (i, 0)),
            pl.BlockSpec((1, k, tn), lambda i, j: (layer, 0, j)),
        ],
        out_specs=pl.BlockSpec((tm, tn), lambda i, j: (i, j)),
        out_shape=jax.ShapeDtypeStruct((m, n), F32),
        compiler_params=_params("parallel", "arbitrary"),
    )(x, a, w)


def _mlp_kernel(x_ref, g_ref, wu_ref, wd_ref, gf_ref, o_ref, h_ref, *, final_norm):
    f = pl.program_id(1)

    @pl.when(f == 0)
    def _():
        x = x_ref[...]
        ms = jnp.mean(x * x, axis=-1, keepdims=True)
        h_ref[...] = (x * lax.rsqrt(ms + EPS) * g_ref[...]).astype(BF16)
        o_ref[...] = x

    up = jnp.dot(h_ref[...], wu_ref[0], preferred_element_type=F32)
    act = jnp.square(jnp.maximum(up, 0.0)).astype(BF16)
    o_ref[...] += jnp.dot(act, wd_ref[0], preferred_element_type=F32)

    if final_norm:
        @pl.when(f == pl.num_programs(1) - 1)
        def _():
            y = o_ref[...]
            ms = jnp.mean(y * y, axis=-1, keepdims=True)
            o_ref[...] = y * lax.rsqrt(ms + EPS) * gf_ref[...]


def _mlp(x, g, w_up, w_down, g_final, final_norm, layer, tm=1024, tf=512):
    m, d = x.shape
    dff = w_up.shape[2]
    return pl.pallas_call(
        functools.partial(_mlp_kernel, final_norm=final_norm),
        grid=(m // tm, dff // tf),
        in_specs=[
            pl.BlockSpec((tm, d), lambda i, f: (i, 0)),
            pl.BlockSpec((1, d), lambda i, f: (0, 0)),
            pl.BlockSpec((1, d, tf), lambda i, f: (layer, 0, f)),
            pl.BlockSpec((1, tf, d), lambda i, f: (layer, f, 0)),
            pl.BlockSpec((1, d), lambda i, f: (0, 0)),
        ],
        out_specs=pl.BlockSpec((tm, d), lambda i, f: (i, 0)),
        out_shape=jax.ShapeDtypeStruct((m, d), F32),
        scratch_shapes=[pltpu.VMEM((tm, d), BF16)],
        compiler_params=_params("parallel", "arbitrary"),
    )(x, g.reshape(1, d), w_up, w_down, g_final.reshape(1, d))


def _shr(x, pow2):
    assert pow2 & (pow2 - 1) == 0
    return lax.shift_right_logical(x, pow2.bit_length() - 1)


def _masked_softmax(s, mask):
    sm = jnp.where(mask, s, NEG)
    e = jnp.exp(sm - jnp.max(sm, axis=-1, keepdims=True))
    return e / jnp.sum(e, axis=-1, keepdims=True)


def _select_blocks(imp, qpos, n_loop):
    r, nb = imp.shape
    bidx = lax.broadcasted_iota(jnp.int32, (r, nb), 1)
    cur = _shr(qpos, BLK)
    forced = (bidx == 0) | (bidx == cur) | (bidx == cur - 1)
    cand = bidx * BLK <= qpos
    score = jnp.where(cand, jnp.where(forced, FORCED, imp), NEG)
    rank = jnp.zeros((r, nb), F32)
    for j in range(n_loop):
        c = score[:, j:j + 1]
        beats = (c > score) | ((c == score) & (bidx > j))
        rank = rank + jnp.where(beats, 1.0, 0.0)
    return jnp.where(cand & (rank < TOP_N), 1.0, 0.0)


def _select_blocks_t(imp_t, qpos_t):
    nb, r = imp_t.shape
    bidx = lax.broadcasted_iota(jnp.int32, (nb, r), 0)
    cur = _shr(qpos_t, BLK)
    forced = (bidx == 0) | (bidx == cur) | (bidx == cur - 1)
    cand = bidx * BLK <= qpos_t
    score = jnp.where(cand, jnp.where(forced, FORCED, imp_t), NEG)
    n_chunks = nb // SUBLANES
    chunks = [score[SUBLANES * c:SUBLANES * (c + 1)] for c in range(n_chunks)]
    ranks = [jnp.zeros((SUBLANES, r), F32) for _ in range(n_chunks)]
    sub = lax.broadcasted_iota(jnp.int32, (SUBLANES, r), 0)
    for j in range(nb):
        cj = score[j:j + 1, :]
        for c in range(n_chunks):
            lo = SUBLANES * c
            ge = jnp.where(cj >= chunks[c], 1.0, 0.0)
            gt = jnp.where(cj > chunks[c], 1.0, 0.0)
            if lo > j:
                term = ge
            elif lo + SUBLANES - 1 < j:
                term = gt
            else:
                term = jnp.where(sub > j - lo, ge, gt)
            ranks[c] = ranks[c] + term
    rank = jnp.concatenate(ranks, axis=0)
    return jnp.where(cand & (rank < TOP_N), 1.0, 0.0)


def _block_expand(n_blocks, k0, n_keys):
    b = lax.broadcasted_iota(jnp.int32, (n_blocks, n_keys), 0)
    k = lax.broadcasted_iota(jnp.int32, (n_blocks, n_keys), 1) + k0
    return jnp.where(_shr(k, BLK) == b, 1.0, 0.0).astype(BF16)


def _softmax_keys(scores, mask):
    sm = jnp.where(mask, scores, NEG)
    e = jnp.exp(sm - jnp.max(sm, axis=0, keepdims=True))
    return e / jnp.sum(e, axis=0, keepdims=True)


def _nsa_prompt_kernel(q_ref, kc_ref, vc_ref, ks_ref, vs_ref, kw_ref, vw_ref, g_ref,
                       o_ref, *, seq, heads):
    i = pl.program_id(2)
    rows = GROUP * Q_TILE
    hw = GROUP * HEAD_DIM
    nb = seq // BLK
    qpos = i * Q_TILE + (lax.broadcasted_iota(jnp.int32, (1, rows), 1) & (Q_TILE - 1))
    bcol = lax.broadcasted_iota(jnp.int32, (nb, 1), 0)
    bidx = lax.broadcasted_iota(jnp.int32, (1, nb), 1)
    cmask = bcol * BLK + (BLK - 1) <= qpos
    own = _shr(bidx, Q_TILE // BLK) == i

    def prepare(h):
        qt = q_ref[:, h * hw:(h + 1) * hw]
        q = jnp.concatenate([qt[:, g * HEAD_DIM:(g + 1) * HEAD_DIM] for g in range(GROUP)], axis=0)
        s_c = lax.dot_general(kc_ref[0, h], q, _NT, preferred_element_type=F32)
        p_c = jnp.where(cmask, _softmax_keys(s_c, cmask), 0.0)
        o_cmp = jnp.dot(vc_ref[0, h], p_c.astype(BF16), preferred_element_type=F32)
        imp = p_c[:, 0:Q_TILE]
        for g in range(1, GROUP):
            imp = imp + p_c[:, g * Q_TILE:(g + 1) * Q_TILE]
        sel = _select_blocks_t(imp, qpos[:, 0:Q_TILE]).T
        block_bias = jnp.where((sel > 0.5) & jnp.logical_not(own), 0.0, NEG).astype(BF16)
        q_aug = jnp.concatenate([q, jnp.concatenate([block_bias] * GROUP, axis=0)], axis=1)
        return q, q_aug, o_cmp

    def online_step(carry, scores, v_t):
        m, l, acc = carry
        m_new = jnp.maximum(m, jnp.max(scores, axis=0, keepdims=True))
        alpha = jnp.exp(m - m_new)
        pj = jnp.exp(scores - m_new)
        l = alpha * l + jnp.sum(pj, axis=0, keepdims=True)
        acc = alpha * acc + jnp.dot(v_t, pj.astype(BF16), preferred_element_type=F32)
        return m_new, l, acc

    prepared = [prepare(h) for h in range(heads)]

    def body(j, carries):
        k0 = pl.multiple_of(j * KEY_TILE, KEY_TILE)
        scores = [lax.dot_general(ks_ref[0, h, pl.ds(k0, KEY_TILE), :], prepared[h][1], _NT,
                                  preferred_element_type=F32) for h in range(heads)]
        return tuple(online_step(carries[h], scores[h], vs_ref[0, h, :, pl.ds(k0, KEY_TILE)])
                     for h in range(heads))

    n_tiles = lax.div(i * Q_TILE + KEY_TILE - 1, KEY_TILE)
    init = tuple((jnp.full((1, rows), NEG, F32), jnp.zeros((1, rows), F32),
                  jnp.zeros((HEAD_DIM, rows), F32)) for _ in range(heads))
    carries = lax.fori_loop(0, n_tiles, body, init)

    d0 = pl.multiple_of(i * Q_TILE, Q_TILE)
    causal = d0 + lax.broadcasted_iota(jnp.int32, (Q_TILE, 1), 0) <= qpos
    span = WINDOW + Q_TILE
    start = pl.multiple_of(jnp.maximum(i * Q_TILE - WINDOW, 0), Q_TILE)
    wpos = start + lax.broadcasted_iota(jnp.int32, (span, 1), 0)
    wmask = (wpos <= qpos) & (wpos >= qpos - WINDOW)
    zero_bias = jnp.zeros((rows, nb), BF16)
    sd = [lax.dot_general(ks_ref[0, h, pl.ds(d0, Q_TILE), :],
                          jnp.concatenate([prepared[h][0], zero_bias], axis=1), _NT,
                          preferred_element_type=F32) for h in range(heads)]
    sw = [lax.dot_general(kw_ref[0, h, pl.ds(start, span), :], prepared[h][0], _NT,
                          preferred_element_type=F32) for h in range(heads)]
    for h in range(heads):
        o_cmp = prepared[h][2]
        _, l, acc = online_step(carries[h], jnp.where(causal, sd[h], NEG),
                                vs_ref[0, h, :, pl.ds(d0, Q_TILE)])
        o_sel = acc / l
        vw_t = vw_ref[0, h, :, pl.ds(start, span)]
        o_win = jnp.dot(vw_t, _softmax_keys(sw[h], wmask).astype(BF16), preferred_element_type=F32)

        g_t = g_ref[:, h * LANES:(h + 1) * LANES].T
        for g in range(GROUP):
            sl = slice(g * Q_TILE, (g + 1) * Q_TILE)
            o_g = (g_t[g:g + 1] * o_cmp[:, sl] + g_t[GROUP + g:GROUP + g + 1] * o_sel[:, sl]
                   + g_t[2 * GROUP + g:2 * GROUP + g + 1] * o_win[:, sl])
            c0 = h * hw + g * HEAD_DIM
            o_ref[:, c0:c0 + HEAD_DIM] = o_g.T.astype(o_ref.dtype)


def _nsa_prompt(q, kc, vc, ks, vs, kw, vw, gates, n, seq, heads=4):
    nq = seq // Q_TILE
    nb = seq // BLK
    onehot = (jnp.arange(seq)[:, None] // BLK == jnp.arange(nb)[None, :]).astype(BF16)
    ks = jnp.concatenate([ks, jnp.broadcast_to(onehot, ks.shape[:2] + onehot.shape)], axis=-1)
    row = lambda b, h, i: (b * nq + i, h)
    head = lambda b, h, i: (b, h, 0, 0)
    k_spec = pl.BlockSpec((1, heads, seq, HEAD_DIM), head)
    v_spec = pl.BlockSpec((1, heads, HEAD_DIM, seq), head)
    return pl.pallas_call(
        functools.partial(_nsa_prompt_kernel, seq=seq, heads=heads),
        grid=(n, N_KV // heads, nq),
        in_specs=[
            pl.BlockSpec((Q_TILE, heads * GROUP * HEAD_DIM), row),
            pl.BlockSpec((1, heads, nb, HEAD_DIM), head), pl.BlockSpec((1, heads, HEAD_DIM, nb), head),
            pl.BlockSpec((1, heads, seq, HEAD_DIM + nb), head), v_spec, k_spec, v_spec,
            pl.BlockSpec((Q_TILE, heads * LANES), row),
        ],
        out_specs=pl.BlockSpec((Q_TILE, heads * GROUP * HEAD_DIM), row),
        out_shape=jax.ShapeDtypeStruct((n * seq, N_KV * GROUP * HEAD_DIM), BF16),
        compiler_params=_params("parallel", "parallel", "arbitrary"),
    )(q, kc, vc, ks, vs, kw, vw, gates)


def _nsa_sample_kernel(pt_ref, q_ref, g_ref, *refs, n_pages, page, dec, wb):
    del pt_ref
    cmp_pages = refs[:n_pages]
    cmp_new = refs[n_pages]
    sel_pages = refs[n_pages + 1:2 * n_pages + 1]
    sel_new = refs[2 * n_pages + 1]
    win_ref, win_new, _, o_ref, wo_ref, kc_ref, k_ref, v_ref, kw_ref, vw_ref = refs[2 * n_pages + 2:]
    past = n_pages * page
    rows = N_KV * GROUP * dec
    q = q_ref[0]
    ridx = lax.broadcasted_iota(jnp.int32, (rows, 1), 0)
    qpos = past + (ridx & (dec - 1))

    n_keys = past + page
    for p_i in range(n_pages + 1):
        pg = sel_pages[p_i][0] if p_i < n_pages else sel_new[0]
        k_ref[:, p_i * page:(p_i + 1) * page] = pg[:KV_W].astype(BF16)
        v_ref[:, p_i * page:(p_i + 1) * page] = pg[KV_W:].astype(BF16)
    wn = win_ref[0]
    nw = win_new[0]
    kw_ref[:, 0:wb] = wn[:KV_W].astype(BF16)
    vw_ref[:, 0:wb] = wn[KV_W:].astype(BF16)
    kw_ref[:, wb:wb + page] = nw[:KV_W].astype(BF16)
    vw_ref[:, wb:wb + page] = nw[KV_W:].astype(BF16)

    bpp = page // BLK
    kc_ref[...] = jnp.zeros_like(kc_ref)
    for p_i in range(n_pages):
        kc_ref[bpp * p_i:bpp * (p_i + 1), :] = cmp_pages[p_i][0]
    kc_ref[bpp * n_pages:bpp * n_pages + 1, :] = cmp_new[0, 0:1, :]
    nbp = kc_ref.shape[0]
    s = lax.dot_general(q, kc_ref[:, :KV_W].astype(BF16), _NT, preferred_element_type=F32)
    bidx = lax.broadcasted_iota(jnp.int32, (1, nbp), 1)
    cmask = bidx * BLK + (BLK - 1) <= qpos
    p = jnp.where(cmask, _masked_softmax(s, cmask), 0.0)
    o_cmp = jnp.dot(p.astype(BF16), kc_ref[:, KV_W:].astype(BF16), preferred_element_type=F32)

    imps = []
    for h in range(N_KV):
        base = h * GROUP * dec
        acc = p[base:base + dec]
        for g in range(1, GROUP):
            acc = acc + p[base + g * dec:base + (g + 1) * dec]
        imps.append(acc)
    imp = jnp.concatenate(imps, axis=0)
    qpos_ht = past + (lax.broadcasted_iota(jnp.int32, (N_KV * dec, 1), 0) & (dec - 1))
    sel = _select_blocks(imp, qpos_ht, bpp * n_pages + 1)
    sel_rows = jnp.concatenate(
        [sel[h * dec:(h + 1) * dec] for h in range(N_KV) for _ in range(GROUP)], axis=0)

    sw = jnp.dot(q, kw_ref[...], preferred_element_type=F32)
    wpos = past - wb + lax.broadcasted_iota(jnp.int32, (1, wb + page), 1)
    pw = _masked_softmax(sw, (wpos <= qpos) & (wpos >= qpos - WINDOW))
    o_win = lax.dot_general(pw.astype(BF16), vw_ref[...], _NT, preferred_element_type=F32)

    ss = jnp.dot(q, k_ref[...], preferred_element_type=F32)
    key_sel = jnp.dot(sel_rows.astype(BF16), _block_expand(nbp, 0, n_keys),
                      preferred_element_type=F32)
    kpos = lax.broadcasted_iota(jnp.int32, (1, n_keys), 1)
    ps = _masked_softmax(ss, (key_sel > 0.5) & (kpos <= qpos))
    o_sel = lax.dot_general(ps.astype(BF16), v_ref[...], _NT, preferred_element_type=F32)

    gt = g_ref[0]
    o = gt[:, 0:1] * o_cmp + gt[:, 1:2] * o_sel + gt[:, 2:3] * o_win
    col = lax.broadcasted_iota(jnp.int32, (1, KV_W), 1)
    o = jnp.where(_shr(ridx, GROUP * dec) == _shr(col, HEAD_DIM), o, 0.0)
    hr = GROUP * dec
    out = o[0:hr]
    for h in range(1, N_KV):
        out = out + o[h * hr:(h + 1) * hr]
    o_ref[0] = out

    shifted = pltpu.roll(wn, wb - dec, 1)
    tail = pltpu.roll(nw, page - dec, 1)
    lane = lax.broadcasted_iota(jnp.int32, (1, page), 1)
    wo_ref[0, :, 0:wb - page] = shifted[:, 0:wb - page]
    wo_ref[0, :, wb - page:wb] = jnp.where(lane >= page - dec, tail, shifted[:, wb - page:wb])


def _nsa_sample(page_table, layer, qbd, gates, cmp_phys, cmp_new, sel_cache_t, sel_new,
                win_cache_t, win_new, win_out):
    nseq, n_pages = page_table.shape
    n_phys = cmp_phys.shape[0]
    page = sel_cache_t.shape[2]
    rows = qbd.shape[1]
    dec = rows // (N_KV * GROUP)
    wb = win_cache_t.shape[2]
    bpp = page // BLK
    nbp = LANES
    assert bpp * n_pages + 1 <= nbp and wb == WINDOW and page == LANES and dec <= SUBLANES

    def page_map(p_i, base):
        return lambda b, pt: (base + pt[b * n_pages + p_i], 0, 0)

    seq_map = lambda b, pt: (b, 0, 0)
    in_specs = [
        pl.BlockSpec((1, rows, KV_W), seq_map),
        pl.BlockSpec((1, rows, 8), seq_map),
    ]
    in_specs += [pl.BlockSpec((1, bpp, 2 * KV_W), page_map(p_i, 0)) for p_i in range(n_pages)]
    in_specs += [pl.BlockSpec((1, 8, 2 * KV_W), seq_map)]
    in_specs += [pl.BlockSpec((1, 2 * KV_W, page), page_map(p_i, layer * n_phys))
                 for p_i in range(n_pages)]
    in_specs += [
        pl.BlockSpec((1, 2 * KV_W, page), seq_map),
        pl.BlockSpec((1, 2 * KV_W, wb), lambda b, pt: (layer * nseq + b, 0, 0)),
        pl.BlockSpec((1, 2 * KV_W, page), seq_map),
        pl.BlockSpec(memory_space=pl.ANY),
    ]
    layer_seq_map = lambda b, pt: (layer * nseq + b, 0, 0)
    n_keys = (n_pages + 1) * page
    grid_spec = pltpu.PrefetchScalarGridSpec(
        num_scalar_prefetch=1,
        grid=(nseq,),
        in_specs=in_specs,
        out_specs=[
            pl.BlockSpec((1, GROUP * dec, KV_W), seq_map),
            pl.BlockSpec((1, 2 * KV_W, wb), layer_seq_map),
        ],
        scratch_shapes=[
            pltpu.VMEM((nbp, 2 * KV_W), F32),
            pltpu.VMEM((KV_W, n_keys), BF16),
            pltpu.VMEM((KV_W, n_keys), BF16),
            pltpu.VMEM((KV_W, wb + page), BF16),
            pltpu.VMEM((KV_W, wb + page), BF16),
        ],
    )
    args = (page_table.reshape(-1), qbd, gates, *([cmp_phys] * n_pages), cmp_new,
            *([sel_cache_t] * n_pages), sel_new, win_cache_t, win_new, win_out)
    return pl.pallas_call(
        functools.partial(_nsa_sample_kernel, n_pages=n_pages, page=page, dec=dec, wb=wb),
        grid_spec=grid_spec,
        out_shape=[
            jax.ShapeDtypeStruct((nseq, GROUP * dec, KV_W), F32),
            jax.ShapeDtypeStruct(win_out.shape, F32),
        ],
        input_output_aliases={len(args) - 1: 1},
        compiler_params=_params("arbitrary"),
    )(*args)


def _heads_major(a, n, seq):
    return a.reshape(n, seq, N_KV, HEAD_DIM).transpose(0, 2, 1, 3).astype(BF16)


def _heads_major_t(a, n, seq):
    return a.reshape(n, seq, N_KV, HEAD_DIM).transpose(0, 2, 3, 1).astype(BF16)


def _pad_rows(a, rows):
    return jnp.pad(a, ((0, 0), (0, rows - a.shape[1]), (0, 0)))


def _channel_major(cache):
    depth, a, rows = cache.shape[:3]
    return cache.transpose(0, 1, 3, 4, 5, 2).reshape(depth * a, 2 * KV_W, rows)


def kernel(x_prompt, x_sample, cache_cmp_kv, cache_sel_kv, cache_win_kv, state_pool, page_table, g_mix, w_in, pe_ck, w_ck1, w_ck2, pe_cv, w_cv1, w_cv2, w_pool_grp, pool_scale, w_br_nsa, w_br_pool, w_o, g_mlp, w_up, w_down, g_final):
    n, seq, d = x_prompt.shape
    nseq, dec, _ = x_sample.shape
    depth = w_in.shape[0]
    n_phys, page = cache_cmp_kv.shape[1], cache_cmp_kv.shape[2]
    wb = cache_win_kv.shape[2]
    mp, ms = n * seq, nseq * dec
    nsa_w = N_KV * GROUP * HEAD_DIM
    kv_cols = 2 * KV_W
    n_gate = 3 * N_KV * GROUP
    pool_w = state_pool.shape[-1]
    pool_state = state_pool.shape[2]
    c_q, c_kv, c_ng = nsa_w, nsa_w + 3 * kv_cols, nsa_w + 3 * kv_cols + n_gate
    c_u = c_ng + pool_w

    x = jnp.concatenate([x_prompt.reshape(mp, d), x_sample.reshape(ms, d)], axis=0)
    kv_shape = (2, N_KV, HEAD_DIM)
    outs = [[] for _ in range(8)]
    cmp_cache_t = _channel_major(cache_cmp_kv).reshape(depth * n_phys * kv_cols, page)
    sel_cache_t = _channel_major(cache_sel_kv)
    win_cache_t = _channel_major(cache_win_kv)
    win_out = jnp.zeros(win_cache_t.shape, F32)

    w_t = w_in.transpose(0, 2, 1)
    w_q = (w_t[:, :c_q] * HEAD_DIM ** -0.5).astype(BF16)
    w_kv = w_t[:, c_q:c_kv].astype(BF16)
    w_ng = w_t[:, c_kv:c_ng].reshape(depth, 3, N_KV, GROUP, d).transpose(0, 2, 1, 3, 4)
    w_ng = jnp.pad(w_ng.reshape(depth, N_KV, 3 * GROUP, d), ((0, 0), (0, 0), (0, LANES - 3 * GROUP), (0, 0)))
    w_ng = w_ng.reshape(depth, N_KV * LANES, d).astype(BF16)
    w_u = w_t[:, c_ng:c_u].astype(BF16)
    w_mg = w_t[:, c_u:].astype(BF16)
    w_bn, w_bp, w_ob = w_br_nsa.astype(BF16), w_br_pool.astype(BF16), w_o.astype(BF16)
    w_upb, w_downb = w_up.astype(BF16), w_down.astype(BF16)

    for l in range(depth):
        zkv = _norm_matmul(x, g_mix[l], w_kv, l, F32)
        q = _norm_matmul(x, g_mix[l], w_q, l, BF16)
        u = _norm_matmul(x, g_mix[l], w_u, l, F32)
        ng = _norm_matmul(x, g_mix[l], w_ng, l, F32, sigmoid=True)
        mg = _norm_matmul(x, g_mix[l], w_mg, l, BF16, sigmoid=True)

        cmp_w = (pe_ck[l], w_ck1[l], w_ck2[l], pe_cv[l], w_cv1[l], w_cv2[l])
        cw = _compress_weights(*cmp_w)

        cmp_p = _compress(zkv, mp // BLK, *cw)
        kc_p = _heads_major(cmp_p[:, :KV_W], n, seq // BLK)
        vc_p = _heads_major_t(cmp_p[:, KV_W:], n, seq // BLK)
        zp = zkv[:mp]
        col = lambda c: zp[:, c * KV_W:(c + 1) * KV_W]
        o_nsa_p = _nsa_prompt(q, kc_p, vc_p, _heads_major(col(2), n, seq), _heads_major_t(col(3), n, seq),
                              _heads_major(col(4), n, seq), _heads_major_t(col(5), n, seq), ng, n, seq)

        zs = zkv[mp:].reshape(nseq, dec, 3 * kv_cols)
        new_c = _pad_rows(zs[:, :, :kv_cols], page)
        lanes_t = lambda a: jnp.pad(a.transpose(0, 2, 1), ((0, 0), (0, 0), (0, page - dec)))
        new_s = lanes_t(zs[:, :, kv_cols:2 * kv_cols])
        new_w = lanes_t(zs[:, :, 2 * kv_cols:])
        bpp = page // BLK
        cmp_phys = _compress_pages(cmp_cache_t, l, n_phys, *_compress_page_weights(*cmp_w))
        cmp_phys = cmp_phys.reshape(n_phys, 2, N_KV, bpp, HEAD_DIM).transpose(0, 3, 1, 2, 4)
        cmp_phys = cmp_phys.reshape(n_phys, bpp, kv_cols)
        cmp_new = _compress(new_c.reshape(nseq * page, kv_cols), nseq * bpp, *cw)
        cmp_new = _pad_rows(cmp_new.reshape(nseq, bpp, kv_cols)[:, :1], 8)

        qs = q[mp:].reshape(nseq, dec, N_KV, GROUP, HEAD_DIM)
        eye = jnp.eye(N_KV, dtype=BF16)
        qbd = jnp.einsum('bthgd,hk->bhgtkd', qs, eye).reshape(nseq, N_KV * GROUP * dec, KV_W)
        gs = ng[mp:].reshape(nseq, dec, N_KV, LANES)[..., :3 * GROUP].reshape(nseq, dec, N_KV, 3, GROUP)
        gs = gs.transpose(0, 2, 4, 1, 3).reshape(nseq, N_KV * GROUP * dec, 3)
        gs = jnp.pad(gs, ((0, 0), (0, 0), (0, 5)))
        o_s, win_out = _nsa_sample(page_table, l, qbd, gs, cmp_phys, cmp_new, sel_cache_t, new_s,
                                   win_cache_t, new_w, win_out)
        o_nsa_s = o_s.reshape(nseq, GROUP, dec, N_KV, HEAD_DIM).transpose(0, 2, 3, 1, 4)
        o_nsa = jnp.concatenate([o_nsa_p, o_nsa_s.reshape(ms, nsa_w).astype(BF16)], axis=0)

        w_grp = w_pool_grp[l].astype(BF16)
        u_p, u_s = u[:mp], u[mp:].reshape(nseq, dec, pool_w)
        o_pool_p = _pool_mix(u_p, w_grp, pool_scale[l], 512, seq // 512)
        hist = jnp.pad(state_pool[l], ((0, 0), (POOL_HALO - pool_state, 0), (0, 0)))
        xx_s = jnp.concatenate([hist, u_s], axis=1)
        rows_s = POOL_HALO + dec
        o_pool_s = _pool_mix(xx_s.reshape(nseq * rows_s, pool_w), w_grp, pool_scale[l],
                             16 * rows_s, None)
        o_pool_s = o_pool_s.reshape(nseq, rows_s, pool_w)[:, POOL_HALO:].reshape(ms, pool_w)
        o_pool = jnp.concatenate([o_pool_p, o_pool_s], axis=0)

        m = _gated_branches(o_nsa, o_pool, w_bn, w_bp, mg, l)
        x1 = _resid_matmul(x, m, w_ob, l)
        x = _mlp(x1, g_mlp[l], w_upb, w_downb, g_final, l == depth - 1, l)

        zp6 = zp.reshape(n, seq, 3, *kv_shape)
        outs[0].append(zp6[:, :, 0])
        outs[1].append(zp6[:, :, 1])
        outs[2].append(zp6[:, -min(WINDOW, seq):, 2])
        outs[3].append(u_p.reshape(n, seq, pool_w)[:, -pool_state:])
        zs6 = zs.reshape(nseq, dec, 3, *kv_shape)
        outs[4].append(zs6[:, :, 0])
        outs[5].append(zs6[:, :, 1])
        outs[7].append(xx_s[:, -pool_state:])

    y_prompt = x[:mp].reshape(n, seq, d)
    y_sample = x[mp:].reshape(nseq, dec, d)
    s_win = win_out.reshape(depth, nseq, *kv_shape, wb).transpose(0, 1, 5, 2, 3, 4)
    stacked = [s_win if i == 6 else jnp.stack(o) for i, o in enumerate(outs)]
    return (y_prompt, y_sample) + tuple(stacked)
```

```python
import functools

import jax
import jax.numpy as jnp
from jax import lax
from jax.experimental import pallas as pl
from jax.experimental.pallas import tpu as pltpu

F32 = jnp.float32
BF16 = jnp.bfloat16

N_KV = 4
GROUP = 4
HEAD_DIM = 64
KV_W = N_KV * HEAD_DIM
BLK = 64
TOP_N = 16
WINDOW = 512
Q_TILE = 128
KEY_TILE = 512
POOL_WINDOWS = (2, 4, 8, 16)
POOL_HALO = 16
EPS = 1e-6
NEG = -1e30
FORCED = 1e4
LANES = 128
SUBLANES = 8
VMEM_LIMIT = 56 * 1024 * 1024

_NT = (((1,), (1,)), ((), ()))


def _params(*sem):
    return pltpu.CompilerParams(dimension_semantics=sem, vmem_limit_bytes=VMEM_LIMIT)


def _norm_matmul_kernel(x_ref, g_ref, w_ref, o_ref, xn_ref, *, sigmoid_tiles):
    j = pl.program_id(1)

    @pl.when(j == 0)
    def _():
        x = x_ref[...]
        ms = jnp.mean(x * x, axis=-1, keepdims=True)
        xn_ref[...] = (x * lax.rsqrt(ms + EPS) * g_ref[...]).astype(BF16)

    z = lax.dot_general(xn_ref[...], w_ref[0], _NT, preferred_element_type=F32)
    first_sigmoid = pl.num_programs(1) - sigmoid_tiles

    @pl.when(j < first_sigmoid)
    def _():
        o_ref[...] = z.astype(o_ref.dtype)

    @pl.when(j >= first_sigmoid)
    def _():
        o_ref[...] = jax.nn.sigmoid(z).astype(o_ref.dtype)


def _norm_matmul(x, g, w_t, layer, out_dtype, sigmoid_tiles=0, tm=1024, tn=512):
    m, d = x.shape
    n = w_t.shape[1]
    tm, tn = min(tm, m), min(tn, n)
    return pl.pallas_call(
        functools.partial(_norm_matmul_kernel, sigmoid_tiles=sigmoid_tiles),
        grid=(m // tm, n // tn),
        in_specs=[
            pl.BlockSpec((tm, d), lambda i, j: (i, 0)),
            pl.BlockSpec((1, d), lambda i, j: (0, 0)),
            pl.BlockSpec((1, tn, d), lambda i, j: (layer, j, 0)),
        ],
        out_specs=pl.BlockSpec((tm, tn), lambda i, j: (i, j)),
        out_shape=jax.ShapeDtypeStruct((m, n), out_dtype),
        scratch_shapes=[pltpu.VMEM((tm, d), BF16)],
        compiler_params=_params("parallel", "arbitrary"),
    )(x, g.reshape(1, d), w_t)


def _compress_kernel(x_ref, pe_ref, w1_ref, w2_ref, o_ref, *, tb):
    acc = jnp.zeros((tb, 2 * LANES), F32)
    for r in range(BLK):
        xr = x_ref[pl.ds(r, tb, stride=BLK), :] + pe_ref[0, r:r + 1, :]
        acc = acc + jnp.dot(xr.astype(BF16), w1_ref[0, r], preferred_element_type=F32)
    hid = jax.nn.gelu(acc)
    o_ref[...] = jnp.dot(hid.astype(BF16), w2_ref[0], preferred_element_type=F32)


def _compress(rows, n_blocks, pe2, w1bd, w2bd, col0=0, tb=128):
    assert col0 % LANES == 0
    return pl.pallas_call(
        functools.partial(_compress_kernel, tb=tb),
        grid=(n_blocks // tb, 4),
        in_specs=[
            pl.BlockSpec((tb * BLK, LANES), lambda i, c: (i, col0 // LANES + c)),
            pl.BlockSpec((1, BLK, LANES), lambda i, c: (c // 2, 0, 0)),
            pl.BlockSpec((1, BLK, LANES, 2 * LANES), lambda i, c: (c // 2, 0, 0, 0)),
            pl.BlockSpec((1, 2 * LANES, LANES), lambda i, c: (c // 2, 0, 0)),
        ],
        out_specs=pl.BlockSpec((tb, LANES), lambda i, c: (i, c)),
        out_shape=jax.ShapeDtypeStruct((n_blocks, 4 * LANES), F32),
        compiler_params=_params("parallel", "arbitrary"),
    )(rows, pe2, w1bd, w2bd)


def _compress_weights(pe_k, w1_k, w2_k, pe_v, w1_v, w2_v):
    def one(pe, w1, w2):
        hid = w1.shape[1]
        w1r = w1.reshape(BLK, HEAD_DIM, hid)
        z = jnp.zeros_like(w1r)
        w1bd = jnp.concatenate(
            [jnp.concatenate([w1r, z], axis=2), jnp.concatenate([z, w1r], axis=2)], axis=1)
        z2 = jnp.zeros_like(w2)
        w2bd = jnp.concatenate(
            [jnp.concatenate([w2, z2], axis=1), jnp.concatenate([z2, w2], axis=1)], axis=0)
        return jnp.concatenate([pe, pe], axis=1), w1bd.astype(BF16), w2bd.astype(BF16)

    k, v = one(pe_k, w1_k, w2_k), one(pe_v, w1_v, w2_v)
    return tuple(jnp.stack([a, b]) for a, b in zip(k, v))


def _compress_pages_kernel(pt_ref, *refs, n_pages):
    del pt_ref
    pages = refs[:n_pages]
    pe_ref, w1_ref, w2_ref, o_ref = refs[n_pages:]
    per_page = 2 * N_KV
    rows = per_page * n_pages
    ridx = lax.broadcasted_iota(jnp.int32, (rows, 1), 0)
    is_v = (_shr(ridx, N_KV) & 1) == 1
    acc = jnp.zeros((rows, 2 * LANES), F32)
    for d in range(HEAD_DIM):
        a = jnp.concatenate([pg[0, pl.ds(d, per_page, stride=HEAD_DIM), :] for pg in pages], axis=0)
        ak = jnp.where(is_v, 0.0, a + pe_ref[0, d:d + 1, :])
        av = jnp.where(is_v, a + pe_ref[1, d:d + 1, :], 0.0)
        lhs = jnp.concatenate([ak, av], axis=1).astype(BF16)
        acc = acc + jnp.dot(lhs, w1_ref[d], preferred_element_type=F32)
    hid = jax.nn.gelu(acc)
    lhs = jnp.concatenate([jnp.where(is_v, 0.0, hid), jnp.where(is_v, hid, 0.0)], axis=1)
    o_ref[...] = jnp.dot(lhs.astype(BF16), w2_ref[...], preferred_element_type=F32)


def _compress_pages(cache_t, page_table, layer, n_phys, pe_t, w1p, w2p):
    nseq, n_pages = page_table.shape
    rows = 2 * N_KV * n_pages

    def page_map(p_i):
        return lambda b, pt: (layer * n_phys + pt[b * n_pages + p_i], 0, 0)

    in_specs = [pl.BlockSpec((1,) + cache_t.shape[1:], page_map(p_i)) for p_i in range(n_pages)]
    in_specs += [
        pl.BlockSpec(pe_t.shape, lambda b, pt: (0, 0, 0)),
        pl.BlockSpec(w1p.shape, lambda b, pt: (0, 0, 0)),
        pl.BlockSpec(w2p.shape, lambda b, pt: (0, 0)),
    ]
    grid_spec = pltpu.PrefetchScalarGridSpec(
        num_scalar_prefetch=1,
        grid=(nseq,),
        in_specs=in_specs,
        out_specs=pl.BlockSpec((rows, LANES), lambda b, pt: (b, 0)),
    )
    return pl.pallas_call(
        functools.partial(_compress_pages_kernel, n_pages=n_pages),
        grid_spec=grid_spec,
        out_shape=jax.ShapeDtypeStruct((nseq * rows, LANES), F32),
        compiler_params=_params("arbitrary"),
    )(page_table.reshape(-1), *([cache_t] * n_pages), pe_t, w1p, w2p)


def _compress_page_weights(pe_k, w1_k, w2_k, pe_v, w1_v, w2_v):
    eye = jnp.eye(2, dtype=F32)
    hid = w1_k.shape[1]

    def w1p(w1):
        w1d = w1.reshape(BLK, HEAD_DIM, hid).transpose(1, 0, 2)
        return jnp.einsum('drj,bc->dbrcj', w1d, eye)

    def w2p(w2):
        return jnp.einsum('je,bc->bjce', w2, eye)

    w1 = jnp.stack([w1p(w1_k), w1p(w1_v)], axis=1).reshape(HEAD_DIM, 4 * BLK, 2 * hid)
    w2 = jnp.stack([w2p(w2_k), w2p(w2_v)]).reshape(4 * hid, 2 * HEAD_DIM)
    pe_t = jnp.stack([jnp.tile(pe_k.T, (1, 2)), jnp.tile(pe_v.T, (1, 2))])
    return pe_t, w1.astype(BF16), w2.astype(BF16)


def _pool_kernel(u_ref, halo_ref, w_ref, s_ref, o_ref, *, tt, tiles_per_seq):
    i = pl.program_id(0)
    u = u_ref[...]
    halo = halo_ref[...]
    if tiles_per_seq is not None:
        halo = jnp.where(i % tiles_per_seq == 0, 0.0, halo)
        pos = (i % tiles_per_seq) * tt + lax.broadcasted_iota(jnp.int32, (tt, 1), 0)
    a = jnp.concatenate([halo, u], axis=0)
    gw = u.shape[1] // len(POOL_WINDOWS)
    for gi, w in enumerate(POOL_WINDOWS):
        b = a[:, gi * gw:(gi + 1) * gw]
        width = 1
        while width < w:
            b = b[:b.shape[0] - width] + b[width:]
            width *= 2
        lo = POOL_HALO + 1 - w
        win = b[lo:lo + tt]
        if tiles_per_seq is None:
            mean = win / float(w)
        else:
            mean = win / jnp.minimum(pos + 1, w).astype(F32)
        d = mean - u[:, gi * gw:(gi + 1) * gw]
        y = jnp.dot(d.astype(BF16), w_ref[gi], preferred_element_type=F32)
        o_ref[:, gi * gw:(gi + 1) * gw] = (y * s_ref[:, gi * gw:(gi + 1) * gw]).astype(o_ref.dtype)


def _pool_mix(u, m, w_grp, scale, tt, tiles_per_seq, col=0):
    width = scale.shape[0]
    hb = tt // POOL_HALO
    return pl.pallas_call(
        functools.partial(_pool_kernel, tt=tt, tiles_per_seq=tiles_per_seq),
        grid=(m // tt,),
        in_specs=[
            pl.BlockSpec((tt, width), lambda i: (i, col)),
            pl.BlockSpec((POOL_HALO, width), lambda i: (jnp.maximum(i * hb - 1, 0), col)),
            pl.BlockSpec(w_grp.shape, lambda i: (0, 0, 0)),
            pl.BlockSpec((1, width), lambda i: (0, 0)),
        ],
        out_specs=pl.BlockSpec((tt, width), lambda i: (i, 0)),
        out_shape=jax.ShapeDtypeStruct((m, width), BF16),
        compiler_params=_params("parallel"),
    )(u, u, w_grp, scale.reshape(1, width))


def _gated_branch_kernel(a_ref, b_ref, wa_ref, wb_ref, x_ref, g_ref, wga_ref, wgb_ref, o_ref, xn_ref):
    @pl.when(pl.program_id(1) == 0)
    def _():
        x = x_ref[...]
        ms = jnp.mean(x * x, axis=-1, keepdims=True)
        xn_ref[...] = (x * lax.rsqrt(ms + EPS) * g_ref[...]).astype(BF16)

    xn = xn_ref[...]
    ga = jax.nn.sigmoid(lax.dot_general(xn, wga_ref[0], _NT, preferred_element_type=F32))
    gb = jax.nn.sigmoid(lax.dot_general(xn, wgb_ref[0], _NT, preferred_element_type=F32))
    a = jnp.dot(a_ref[...], wa_ref[0], preferred_element_type=F32)
    b = jnp.dot(b_ref[...], wb_ref[0], preferred_element_type=F32)
    o_ref[...] = (ga * a + gb * b).astype(o_ref.dtype)


def _gated_branches(a, b, wa, wb, x, g, wg_t, layer, tm=1024, tn=512):
    m, k = a.shape
    d = x.shape[1]
    n = wa.shape[2]
    nj = n // tn
    return pl.pallas_call(
        _gated_branch_kernel,
        grid=(m // tm, nj),
        in_specs=[
            pl.BlockSpec((tm, k), lambda i, j: (i, 0)),
            pl.BlockSpec((tm, k), lambda i, j: (i, 0)),
            pl.BlockSpec((1, k, tn), lambda i, j: (layer, 0, j)),
            pl.BlockSpec((1, k, tn), lambda i, j: (layer, 0, j)),
            pl.BlockSpec((tm, d), lambda i, j: (i, 0)),
            pl.BlockSpec((1, d), lambda i, j: (0, 0)),
            pl.BlockSpec((1, tn, d), lambda i, j: (layer, j, 0)),
            pl.BlockSpec((1, tn, d), lambda i, j: (layer, j + nj, 0)),
        ],
        out_specs=pl.BlockSpec((tm, tn), lambda i, j: (i, j)),
        out_shape=jax.ShapeDtypeStruct((m, n), BF16),
        scratch_shapes=[pltpu.VMEM((tm, d), BF16)],
        compiler_params=_params("parallel", "arbitrary"),
    )(a, b, wa, wb, x, g.reshape(1, d), wg_t, wg_t)


def _resid_matmul_kernel(x_ref, a_ref, w_ref, o_ref):
    o_ref[...] = x_ref[...] + jnp.dot(a_ref[...], w_ref[0], preferred_element_type=F32)


def _resid_matmul(x, a, w, layer, tm=1024, tn=512):
    m, k = a.shape
    n = w.shape[2]
    return pl.pallas_call(
        _resid_matmul_kernel,
        grid=(m // tm, n // tn),
        in_specs=[
            pl.BlockSpec((tm, tn), lambda i, j: (i, j)),
            pl.BlockSpec((tm, k), lambda i, j: (i, 0)),
            pl.BlockSpec((1, k, tn), lambda i, j: (layer, 0, j)),
        ],
        out_specs=pl.BlockSpec((tm, tn), lambda i, j: (i, j)),
        out_shape=jax.ShapeDtypeStruct((m, n), F32),
        compiler_params=_params("parallel", "arbitrary"),
    )(x, a, w)


def _mlp_kernel(x_ref, g_ref, wu_ref, wd_ref, gf_ref, o_ref, h_ref, *, final_norm):
    f = pl.program_id(1)

    @pl.when(f == 0)
    def _():
        x = x_ref[...]
        ms = jnp.mean(x * x, axis=-1, keepdims=True)
        h_ref[...] = (x * lax.rsqrt(ms + EPS) * g_ref[...]).astype(BF16)
        o_ref[...] = x

    up = jnp.dot(h_ref[...], wu_ref[0], preferred_element_type=F32)
    act = jnp.square(jnp.maximum(up, 0.0)).astype(BF16)
    o_ref[...] += jnp.dot(act, wd_ref[0], preferred_element_type=F32)

    if final_norm:
        @pl.when(f == pl.num_programs(1) - 1)
        def _():
            y = o_ref[...]
            ms = jnp.mean(y * y, axis=-1, keepdims=True)
            o_ref[...] = y * lax.rsqrt(ms + EPS) * gf_ref[...]


def _mlp(x, g, w_up, w_down, g_final, final_norm, layer, tm=1024, tf=512):
    m, d = x.shape
    dff = w_up.shape[2]
    return pl.pallas_call(
        functools.partial(_mlp_kernel, final_norm=final_norm),
        grid=(m // tm, dff // tf),
        in_specs=[
            pl.BlockSpec((tm, d), lambda i, f: (i, 0)),
            pl.BlockSpec((1, d), lambda i, f: (0, 0)),
            pl.BlockSpec((1, d, tf), lambda i, f: (layer, 0, f)),
            pl.BlockSpec((1, tf, d), lambda i, f: (layer, f, 0)),
            pl.BlockSpec((1, d), lambda i, f: (0, 0)),
        ],
        out_specs=pl.BlockSpec((tm, d), lambda i, f: (i, 0)),
        out_shape=jax.ShapeDtypeStruct((m, d), F32),
        scratch_shapes=[pltpu.VMEM((tm, d), BF16)],
        compiler_params=_params("parallel", "arbitrary"),
    )(x, g.reshape(1, d), w_up, w_down, g_final.reshape(1, d))


def _shr(x, pow2):
    assert pow2 & (pow2 - 1) == 0
    return lax.shift_right_logical(x, pow2.bit_length() - 1)


def _masked_softmax(s, mask):
    sm = jnp.where(mask, s, NEG)
    e = jnp.exp(sm - jnp.max(sm, axis=-1, keepdims=True))
    return e / jnp.sum(e, axis=-1, keepdims=True)


def _select_blocks(imp, qpos, n_loop):
    r, nb = imp.shape
    bidx = lax.broadcasted_iota(jnp.int32, (r, nb), 1)
    cur = _shr(qpos, BLK)
    forced = (bidx == 0) | (bidx == cur) | (bidx == cur - 1)
    cand = bidx * BLK <= qpos
    score = jnp.where(cand, jnp.where(forced, FORCED, imp), NEG)
    rank = jnp.zeros((r, nb), F32)
    for j in range(n_loop):
        c = score[:, j:j + 1]
        beats = (c > score) | ((c == score) & (bidx > j))
        rank = rank + jnp.where(beats, 1.0, 0.0)
    return jnp.where(cand & (rank < TOP_N), 1.0, 0.0)


def _select_blocks_t(imp_t, qpos_t):
    nb, r = imp_t.shape
    bidx = lax.broadcasted_iota(jnp.int32, (nb, r), 0)
    cur = _shr(qpos_t, BLK)
    forced = (bidx == 0) | (bidx == cur) | (bidx == cur - 1)
    cand = bidx * BLK <= qpos_t
    score = jnp.where(cand, jnp.where(forced, FORCED, imp_t), NEG)
    n_chunks = nb // SUBLANES
    chunks = [score[SUBLANES * c:SUBLANES * (c + 1)] for c in range(n_chunks)]
    ranks = [jnp.zeros((SUBLANES, r), F32) for _ in range(n_chunks)]
    sub = lax.broadcasted_iota(jnp.int32, (SUBLANES, r), 0)
    for j in range(nb):
        cj = score[j:j + 1, :]
        for c in range(n_chunks):
            lo = SUBLANES * c
            ge = jnp.where(cj >= chunks[c], 1.0, 0.0)
            gt = jnp.where(cj > chunks[c], 1.0, 0.0)
            if lo > j:
                term = ge
            elif lo + SUBLANES - 1 < j:
                term = gt
            else:
                term = jnp.where(sub > j - lo, ge, gt)
            ranks[c] = ranks[c] + term
    rank = jnp.concatenate(ranks, axis=0)
    return jnp.where(cand & (rank < TOP_N), 1.0, 0.0)


def _block_expand(n_blocks, k0, n_keys):
    b = lax.broadcasted_iota(jnp.int32, (n_blocks, n_keys), 0)
    k = lax.broadcasted_iota(jnp.int32, (n_blocks, n_keys), 1) + k0
    return jnp.where(_shr(k, BLK) == b, 1.0, 0.0).astype(BF16)


def _softmax_keys(scores, mask):
    sm = jnp.where(mask, scores, NEG)
    e = jnp.exp(sm - jnp.max(sm, axis=0, keepdims=True))
    return e / jnp.sum(e, axis=0, keepdims=True)


def _nsa_prompt_kernel(q_ref, kc_ref, vc_ref, ks_ref, vs_ref, kw_ref, vw_ref, g_ref,
                       o_ref, *, seq, heads):
    i = pl.program_id(2)
    rows = GROUP * Q_TILE
    hw = GROUP * HEAD_DIM
    nb = seq // BLK
    qpos = i * Q_TILE + (lax.broadcasted_iota(jnp.int32, (1, rows), 1) & (Q_TILE - 1))
    bcol = lax.broadcasted_iota(jnp.int32, (nb, 1), 0)
    bidx = lax.broadcasted_iota(jnp.int32, (1, nb), 1)
    cmask = bcol * BLK + (BLK - 1) <= qpos
    own = _shr(bidx, Q_TILE // BLK) == i

    def prepare(h):
        qt = q_ref[:, h * hw:(h + 1) * hw].astype(BF16)
        q = jnp.concatenate([qt[:, g * HEAD_DIM:(g + 1) * HEAD_DIM] for g in range(GROUP)], axis=0)
        s_c = lax.dot_general(kc_ref[0, h], q, _NT, preferred_element_type=F32)
        p_c = jnp.where(cmask, _softmax_keys(s_c, cmask), 0.0)
        o_cmp = jnp.dot(vc_ref[0, h], p_c.astype(BF16), preferred_element_type=F32)
        imp = p_c[:, 0:Q_TILE]
        for g in range(1, GROUP):
            imp = imp + p_c[:, g * Q_TILE:(g + 1) * Q_TILE]
        sel = _select_blocks_t(imp, qpos[:, 0:Q_TILE]).T
        block_bias = jnp.where((sel > 0.5) & jnp.logical_not(own), 0.0, NEG).astype(BF16)
        q_aug = jnp.concatenate([q, jnp.concatenate([block_bias] * GROUP, axis=0)], axis=1)
        return q, q_aug, o_cmp

    def online_step(carry, scores, v_t):
        m, l, acc = carry
        m_new = jnp.maximum(m, jnp.max(scores, axis=0, keepdims=True))
        alpha = jnp.exp(m - m_new)
        pj = jnp.exp(scores - m_new)
        l = alpha * l + jnp.sum(pj, axis=0, keepdims=True)
        acc = alpha * acc + jnp.dot(v_t, pj.astype(BF16), preferred_element_type=F32)
        return m_new, l, acc

    prepared = [prepare(h) for h in range(heads)]

    def body(j, carries):
        k0 = pl.multiple_of(j * KEY_TILE, KEY_TILE)
        scores = [lax.dot_general(ks_ref[0, h, pl.ds(k0, KEY_TILE), :], prepared[h][1], _NT,
                                  preferred_element_type=F32) for h in range(heads)]
        return tuple(online_step(carries[h], scores[h], vs_ref[0, h, :, pl.ds(k0, KEY_TILE)])
                     for h in range(heads))

    n_tiles = lax.div(i * Q_TILE + KEY_TILE - 1, KEY_TILE)
    init = tuple((jnp.full((1, rows), NEG, F32), jnp.zeros((1, rows), F32),
                  jnp.zeros((HEAD_DIM, rows), F32)) for _ in range(heads))
    carries = lax.fori_loop(0, n_tiles, body, init)

    d0 = pl.multiple_of(i * Q_TILE, Q_TILE)
    causal = d0 + lax.broadcasted_iota(jnp.int32, (Q_TILE, 1), 0) <= qpos
    span = WINDOW + Q_TILE
    start = pl.multiple_of(jnp.maximum(i * Q_TILE - WINDOW, 0), Q_TILE)
    wpos = start + lax.broadcasted_iota(jnp.int32, (span, 1), 0)
    wmask = (wpos <= qpos) & (wpos >= qpos - WINDOW)
    zero_bias = jnp.zeros((rows, nb), BF16)
    sd = [lax.dot_general(ks_ref[0, h, pl.ds(d0, Q_TILE), :],
                          jnp.concatenate([prepared[h][0], zero_bias], axis=1), _NT,
                          preferred_element_type=F32) for h in range(heads)]
    sw = [lax.dot_general(kw_ref[0, h, pl.ds(start, span), :], prepared[h][0], _NT,
                          preferred_element_type=F32) for h in range(heads)]
    for h in range(heads):
        o_cmp = prepared[h][2]
        _, l, acc = online_step(carries[h], jnp.where(causal, sd[h], NEG),
                                vs_ref[0, h, :, pl.ds(d0, Q_TILE)])
        o_sel = acc / l
        vw_t = vw_ref[0, h, :, pl.ds(start, span)]
        o_win = jnp.dot(vw_t, _softmax_keys(sw[h], wmask).astype(BF16), preferred_element_type=F32)

        g_t = g_ref[:, h * LANES:(h + 1) * LANES].T
        for g in range(GROUP):
            sl = slice(g * Q_TILE, (g + 1) * Q_TILE)
            o_g = (g_t[g:g + 1] * o_cmp[:, sl] + g_t[GROUP + g:GROUP + g + 1] * o_sel[:, sl]
                   + g_t[2 * GROUP + g:2 * GROUP + g + 1] * o_win[:, sl])
            c0 = h * hw + g * HEAD_DIM
            o_ref[:, c0:c0 + HEAD_DIM] = o_g.T.astype(o_ref.dtype)


def _nsa_prompt(q, q_col, kc, vc, ks, vs, kw, vw, gates, gate_col, n, seq, heads=4):
    nq = seq // Q_TILE
    nb = seq // BLK
    onehot = (jnp.arange(seq)[:, None] // BLK == jnp.arange(nb)[None, :]).astype(BF16)
    ks = jnp.concatenate([ks, jnp.broadcast_to(onehot, ks.shape[:2] + onehot.shape)], axis=-1)
    row = lambda b, h, i: (b * nq + i, h)
    head = lambda b, h, i: (b, h, 0, 0)
    steps = N_KV // heads
    k_spec = pl.BlockSpec((1, heads, seq, HEAD_DIM), head)
    v_spec = pl.BlockSpec((1, heads, HEAD_DIM, seq), head)
    return pl.pallas_call(
        functools.partial(_nsa_prompt_kernel, seq=seq, heads=heads),
        grid=(n, N_KV // heads, nq),
        in_specs=[
            pl.BlockSpec((Q_TILE, heads * GROUP * HEAD_DIM), lambda b, h, i: (b * nq + i, q_col * steps + h)),
            pl.BlockSpec((1, heads, nb, HEAD_DIM), head), pl.BlockSpec((1, heads, HEAD_DIM, nb), head),
            pl.BlockSpec((1, heads, seq, HEAD_DIM + nb), head), v_spec, k_spec, v_spec,
            pl.BlockSpec((Q_TILE, heads * LANES), lambda b, h, i: (b * nq + i, gate_col * steps + h)),
        ],
        out_specs=pl.BlockSpec((Q_TILE, heads * GROUP * HEAD_DIM), row),
        out_shape=jax.ShapeDtypeStruct((n * seq, N_KV * GROUP * HEAD_DIM), BF16),
        compiler_params=_params("parallel", "parallel", "arbitrary"),
    )(q, kc, vc, ks, vs, kw, vw, gates)


def _nsa_sample_kernel(pt_ref, q_ref, g_ref, *refs, n_pages, page, dec, wb):
    del pt_ref
    cmp_seq, cmp_new = refs[:2]
    sel_pages = refs[2:n_pages + 2]
    sel_new = refs[n_pages + 2]
    win_ref, win_new, _, o_ref, wo_ref, kc_ref, k_ref, v_ref, kw_ref, vw_ref = refs[n_pages + 3:]
    past = n_pages * page
    rows = N_KV * GROUP * dec
    q = q_ref[0]
    ridx = lax.broadcasted_iota(jnp.int32, (rows, 1), 0)
    qpos = past + (ridx & (dec - 1))

    n_keys = past + page
    for p_i in range(n_pages + 1):
        pg = sel_pages[p_i][0] if p_i < n_pages else sel_new[0]
        k_ref[:, p_i * page:(p_i + 1) * page] = pg[:KV_W].astype(BF16)
        v_ref[:, p_i * page:(p_i + 1) * page] = pg[KV_W:].astype(BF16)
    wn = win_ref[0]
    nw = win_new[0]
    kw_ref[:, 0:wb] = wn[:KV_W].astype(BF16)
    vw_ref[:, 0:wb] = wn[KV_W:].astype(BF16)
    kw_ref[:, wb:wb + page] = nw[:KV_W].astype(BF16)
    vw_ref[:, wb:wb + page] = nw[KV_W:].astype(BF16)

    bpp = page // BLK
    kc_ref[...] = jnp.zeros_like(kc_ref)
    kc_ref[0:bpp * n_pages, :] = cmp_seq[0]
    kc_ref[bpp * n_pages:bpp * n_pages + 1, :] = cmp_new[0, 0:1, :]
    nbp = kc_ref.shape[0]
    s = lax.dot_general(q, kc_ref[:, :KV_W].astype(BF16), _NT, preferred_element_type=F32)
    bidx = lax.broadcasted_iota(jnp.int32, (1, nbp), 1)
    cmask = bidx * BLK + (BLK - 1) <= qpos
    p = jnp.where(cmask, _masked_softmax(s, cmask), 0.0)
    o_cmp = jnp.dot(p.astype(BF16), kc_ref[:, KV_W:].astype(BF16), preferred_element_type=F32)

    imps = []
    for h in range(N_KV):
        base = h * GROUP * dec
        acc = p[base:base + dec]
        for g in range(1, GROUP):
            acc = acc + p[base + g * dec:base + (g + 1) * dec]
        imps.append(acc)
    imp = jnp.concatenate(imps, axis=0)
    qpos_ht = past + (lax.broadcasted_iota(jnp.int32, (N_KV * dec, 1), 0) & (dec - 1))
    sel = _select_blocks(imp, qpos_ht, bpp * n_pages + 1)
    sel_rows = jnp.concatenate(
        [sel[h * dec:(h + 1) * dec] for h in range(N_KV) for _ in range(GROUP)], axis=0)

    sw = jnp.dot(q, kw_ref[...], preferred_element_type=F32)
    wpos = past - wb + lax.broadcasted_iota(jnp.int32, (1, wb + page), 1)
    pw = _masked_softmax(sw, (wpos <= qpos) & (wpos >= qpos - WINDOW))
    o_win = lax.dot_general(pw.astype(BF16), vw_ref[...], _NT, preferred_element_type=F32)

    ss = jnp.dot(q, k_ref[...], preferred_element_type=F32)
    key_sel = jnp.dot(sel_rows.astype(BF16), _block_expand(nbp, 0, n_keys),
                      preferred_element_type=F32)
    kpos = lax.broadcasted_iota(jnp.int32, (1, n_keys), 1)
    ps = _masked_softmax(ss, (key_sel > 0.5) & (kpos <= qpos))
    o_sel = lax.dot_general(ps.astype(BF16), v_ref[...], _NT, preferred_element_type=F32)

    gt = g_ref[0]
    o = gt[:, 0:1] * o_cmp + gt[:, 1:2] * o_sel + gt[:, 2:3] * o_win
    col = lax.broadcasted_iota(jnp.int32, (1, KV_W), 1)
    o = jnp.where(_shr(ridx, GROUP * dec) == _shr(col, HEAD_DIM), o, 0.0)
    hr = GROUP * dec
    out = o[0:hr]
    for h in range(1, N_KV):
        out = out + o[h * hr:(h + 1) * hr]
    o_ref[0] = out

    shifted = pltpu.roll(wn, wb - dec, 1)
    tail = pltpu.roll(nw, page - dec, 1)
    lane = lax.broadcasted_iota(jnp.int32, (1, page), 1)
    wo_ref[0, :, 0:wb - page] = shifted[:, 0:wb - page]
    wo_ref[0, :, wb - page:wb] = jnp.where(lane >= page - dec, tail, shifted[:, wb - page:wb])


def _nsa_sample(page_table, layer, n_phys, qbd, gates, cmp_seq, cmp_new, sel_cache_t, sel_new,
                win_cache_t, win_new, win_out):
    nseq, n_pages = page_table.shape
    page = sel_cache_t.shape[2]
    rows = qbd.shape[1]
    dec = rows // (N_KV * GROUP)
    wb = win_cache_t.shape[2]
    bpp = page // BLK
    nbp = LANES
    assert bpp * n_pages + 1 <= nbp and wb == WINDOW and page == LANES and dec <= SUBLANES

    def page_map(p_i, base):
        return lambda b, pt: (base + pt[b * n_pages + p_i], 0, 0)

    seq_map = lambda b, pt: (b, 0, 0)
    in_specs = [
        pl.BlockSpec((1, rows, KV_W), seq_map),
        pl.BlockSpec((1, rows, 8), seq_map),
    ]
    in_specs += [pl.BlockSpec((1, bpp * n_pages, 2 * KV_W), seq_map),
                 pl.BlockSpec((1, 8, 2 * KV_W), seq_map)]
    in_specs += [pl.BlockSpec((1, 2 * KV_W, page), page_map(p_i, layer * n_phys))
                 for p_i in range(n_pages)]
    in_specs += [
        pl.BlockSpec((1, 2 * KV_W, page), seq_map),
        pl.BlockSpec((1, 2 * KV_W, wb), lambda b, pt: (layer * nseq + b, 0, 0)),
        pl.BlockSpec((1, 2 * KV_W, page), seq_map),
        pl.BlockSpec(memory_space=pl.ANY),
    ]
    layer_seq_map = lambda b, pt: (layer * nseq + b, 0, 0)
    n_keys = (n_pages + 1) * page
    grid_spec = pltpu.PrefetchScalarGridSpec(
        num_scalar_prefetch=1,
        grid=(nseq,),
        in_specs=in_specs,
        out_specs=[
            pl.BlockSpec((1, GROUP * dec, KV_W), seq_map),
            pl.BlockSpec((1, 2 * KV_W, wb), layer_seq_map),
        ],
        scratch_shapes=[
            pltpu.VMEM((nbp, 2 * KV_W), F32),
            pltpu.VMEM((KV_W, n_keys), BF16),
            pltpu.VMEM((KV_W, n_keys), BF16),
            pltpu.VMEM((KV_W, wb + page), BF16),
            pltpu.VMEM((KV_W, wb + page), BF16),
        ],
    )
    args = (page_table.reshape(-1), qbd, gates, cmp_seq, cmp_new,
            *([sel_cache_t] * n_pages), sel_new, win_cache_t, win_new, win_out)
    return pl.pallas_call(
        functools.partial(_nsa_sample_kernel, n_pages=n_pages, page=page, dec=dec, wb=wb),
        grid_spec=grid_spec,
        out_shape=[
            jax.ShapeDtypeStruct((nseq, GROUP * dec, KV_W), F32),
            jax.ShapeDtypeStruct(win_out.shape, F32),
        ],
        input_output_aliases={len(args) - 1: 1},
        compiler_params=_params("arbitrary"),
    )(*args)


def _heads_major(a, n, seq):
    return a.reshape(n, seq, N_KV, HEAD_DIM).transpose(0, 2, 1, 3).astype(BF16)


def _heads_major_t(a, n, seq):
    return a.reshape(n, seq, N_KV, HEAD_DIM).transpose(0, 2, 3, 1).astype(BF16)


def _pad_rows(a, rows):
    return jnp.pad(a, ((0, 0), (0, rows - a.shape[1]), (0, 0)))


def _channel_major(cache):
    depth, a, rows = cache.shape[:3]
    return cache.transpose(0, 1, 3, 4, 5, 2).reshape(depth * a, 2 * KV_W, rows)


def kernel(x_prompt, x_sample, cache_cmp_kv, cache_sel_kv, cache_win_kv, state_pool, page_table, g_mix, w_in, pe_ck, w_ck1, w_ck2, pe_cv, w_cv1, w_cv2, w_pool_grp, pool_scale, w_br_nsa, w_br_pool, w_o, g_mlp, w_up, w_down, g_final):
    n, seq, d = x_prompt.shape
    nseq, dec, _ = x_sample.shape
    depth = w_in.shape[0]
    n_phys, page = cache_cmp_kv.shape[1], cache_cmp_kv.shape[2]
    wb = cache_win_kv.shape[2]
    mp, ms = n * seq, nseq * dec
    nsa_w = N_KV * GROUP * HEAD_DIM
    kv_cols = 2 * KV_W
    n_gate = 3 * N_KV * GROUP
    pool_w = state_pool.shape[-1]
    pool_state = state_pool.shape[2]
    c_q, c_kv, c_ng = nsa_w, nsa_w + 3 * kv_cols, nsa_w + 3 * kv_cols + n_gate
    c_u = c_ng + pool_w

    x = jnp.concatenate([x_prompt.reshape(mp, d), x_sample.reshape(ms, d)], axis=0)
    kv_shape = (2, N_KV, HEAD_DIM)
    outs = [[] for _ in range(8)]
    cmp_cache_t = _channel_major(cache_cmp_kv)
    sel_cache_t = _channel_major(cache_sel_kv)
    win_cache_t = _channel_major(cache_win_kv)
    win_out = jnp.zeros(win_cache_t.shape, F32)

    w_t = w_in.transpose(0, 2, 1)
    w_q = (w_t[:, :c_q] * HEAD_DIM ** -0.5).astype(BF16)
    w_kv = w_t[:, c_q:c_kv].astype(BF16)
    w_ng = w_t[:, c_kv:c_ng].reshape(depth, 3, N_KV, GROUP, d).transpose(0, 2, 1, 3, 4)
    w_ng = jnp.pad(w_ng.reshape(depth, N_KV, 3 * GROUP, d), ((0, 0), (0, 0), (0, LANES - 3 * GROUP), (0, 0)))
    w_ng = w_ng.reshape(depth, N_KV * LANES, d).astype(BF16)
    w_u = w_t[:, c_ng:c_u].astype(BF16)
    w_mg = w_t[:, c_u:].astype(BF16)
    w_cat = jnp.concatenate([w_q, w_u, w_kv, w_ng], axis=1)
    o_u, o_kv, o_ng = nsa_w, nsa_w + pool_w, nsa_w + pool_w + 3 * kv_cols
    assert nsa_w == N_KV * GROUP * HEAD_DIM and o_u % pool_w == 0 and o_ng % (N_KV * LANES) == 0
    w_bn, w_bp, w_ob = w_br_nsa.astype(BF16), w_br_pool.astype(BF16), w_o.astype(BF16)
    w_upb, w_downb = w_up.astype(BF16), w_down.astype(BF16)

    for l in range(depth):
        z = _norm_matmul(x, g_mix[l], w_cat, l, F32, sigmoid_tiles=1)
        zkv = z[:, o_kv:o_ng]
        q = z[:, :nsa_w]
        ng = z[:, o_ng:]

        cmp_w = (pe_ck[l], w_ck1[l], w_ck2[l], pe_cv[l], w_cv1[l], w_cv2[l])
        cw = _compress_weights(*cmp_w)

        cmp_p = _compress(z, mp // BLK, *cw, col0=o_kv)
        kc_p = _heads_major(cmp_p[:, :KV_W], n, seq // BLK)
        vc_p = _heads_major_t(cmp_p[:, KV_W:], n, seq // BLK)
        zp = zkv[:mp]
        col = lambda c: zp[:, c * KV_W:(c + 1) * KV_W]
        o_nsa_p = _nsa_prompt(z, 0, kc_p, vc_p, _heads_major(col(2), n, seq), _heads_major_t(col(3), n, seq),
                              _heads_major(col(4), n, seq), _heads_major_t(col(5), n, seq),
                              z, o_ng // (N_KV * LANES), n, seq)

        zs = zkv[mp:].reshape(nseq, dec, 3 * kv_cols)
        new_c = _pad_rows(zs[:, :, :kv_cols], page)
        lanes_t = lambda a: jnp.pad(a.transpose(0, 2, 1), ((0, 0), (0, 0), (0, page - dec)))
        new_s = lanes_t(zs[:, :, kv_cols:2 * kv_cols])
        new_w = lanes_t(zs[:, :, 2 * kv_cols:])
        bpp = page // BLK
        cmp_seq = _compress_pages(cmp_cache_t, page_table, l, n_phys, *_compress_page_weights(*cmp_w))
        n_pages = page_table.shape[1]
        cmp_seq = cmp_seq.reshape(nseq, n_pages, 2, N_KV, bpp, HEAD_DIM).transpose(0, 1, 4, 2, 3, 5)
        cmp_seq = cmp_seq.reshape(nseq, n_pages * bpp, kv_cols)
        cmp_new = _compress(new_c.reshape(nseq * page, kv_cols), nseq * bpp, *cw)
        cmp_new = _pad_rows(cmp_new.reshape(nseq, bpp, kv_cols)[:, :1], 8)

        qs = q[mp:].astype(BF16).reshape(nseq, dec, N_KV, GROUP, HEAD_DIM)
        eye = jnp.eye(N_KV, dtype=BF16)
        qbd = jnp.einsum('bthgd,hk->bhgtkd', qs, eye).reshape(nseq, N_KV * GROUP * dec, KV_W)
        gs = ng[mp:].reshape(nseq, dec, N_KV, LANES)[..., :3 * GROUP].reshape(nseq, dec, N_KV, 3, GROUP)
        gs = gs.transpose(0, 2, 4, 1, 3).reshape(nseq, N_KV * GROUP * dec, 3)
        gs = jnp.pad(gs, ((0, 0), (0, 0), (0, 5)))
        o_s, win_out = _nsa_sample(page_table, l, n_phys, qbd, gs, cmp_seq, cmp_new, sel_cache_t, new_s,
                                   win_cache_t, new_w, win_out)
        o_nsa_s = o_s.reshape(nseq, GROUP, dec, N_KV, HEAD_DIM).transpose(0, 2, 3, 1, 4)
        o_nsa = jnp.concatenate([o_nsa_p, o_nsa_s.reshape(ms, nsa_w).astype(BF16)], axis=0)

        w_grp = w_pool_grp[l].astype(BF16)
        u = z[:, o_u:o_kv]
        u_p, u_s = u[:mp], u[mp:].reshape(nseq, dec, pool_w)
        o_pool_p = _pool_mix(z, mp, w_grp, pool_scale[l], 512, seq // 512, col=o_u // pool_w)
        hist = jnp.pad(state_pool[l], ((0, 0), (POOL_HALO - pool_state, 0), (0, 0)))
        xx_s = jnp.concatenate([hist, u_s], axis=1)
        rows_s = POOL_HALO + dec
        o_pool_s = _pool_mix(xx_s.reshape(nseq * rows_s, pool_w), nseq * rows_s, w_grp, pool_scale[l],
                             16 * rows_s, None)
        o_pool_s = o_pool_s.reshape(nseq, rows_s, pool_w)[:, POOL_HALO:].reshape(ms, pool_w)
        o_pool = jnp.concatenate([o_pool_p, o_pool_s], axis=0)

        m = _gated_branches(o_nsa, o_pool, w_bn, w_bp, x, g_mix[l], w_mg, l)
        x1 = _resid_matmul(x, m, w_ob, l)
        x = _mlp(x1, g_mlp[l], w_upb, w_downb, g_final, l == depth - 1, l)

        zp6 = zp.reshape(n, seq, 3, *kv_shape)
        outs[0].append(zp6[:, :, 0])
        outs[1].append(zp6[:, :, 1])
        outs[2].append(zp6[:, -min(WINDOW, seq):, 2])
        outs[3].append(u_p.reshape(n, seq, pool_w)[:, -pool_state:])
        zs6 = zs.reshape(nseq, dec, 3, *kv_shape)
        outs[4].append(zs6[:, :, 0])
        outs[5].append(zs6[:, :, 1])
        outs[7].append(xx_s[:, -pool_state:])

    y_prompt = x[:mp].reshape(n, seq, d)
    y_sample = x[mp:].reshape(nseq, dec, d)
    s_win = win_out.reshape(depth, nseq, *kv_shape, wb).transpose(0, 1, 5, 2, 3, 4)
    stacked = [s_win if i == 6 else jnp.stack(o) for i, o in enumerate(outs)]
    return (y_prompt, y_sample) + tuple(stacked)
```

```python
import functools

import jax
import jax.numpy as jnp
from jax import lax
from jax.experimental import pallas as pl
from jax.experimental.pallas import tpu as pltpu

F32 = jnp.float32
BF16 = jnp.bfloat16

N_KV = 4
GROUP = 4
HEAD_DIM = 64
KV_W = N_KV * HEAD_DIM
BLK = 64
TOP_N = 16
WINDOW = 512
Q_TILE = 128
KEY_TILE = 512
POOL_WINDOWS = (2, 4, 8, 16)
POOL_HALO = 16
EPS = 1e-6
NEG = -1e30
FORCED = 1e4
LANES = 128
SUBLANES = 8
VMEM_LIMIT = 56 * 1024 * 1024

_NT = (((1,), (1,)), ((), ()))


def _params(*sem):
    return pltpu.CompilerParams(dimension_semantics=sem, vmem_limit_bytes=VMEM_LIMIT)


def _norm_matmul_kernel(x_ref, g_ref, w_ref, o_ref, xn_ref, *, sigmoid_tiles, onehot_seq):
    i, j = pl.program_id(0), pl.program_id(1)
    tm, tn = o_ref.shape

    @pl.when(j == 0)
    def _():
        x = x_ref[...]
        ms = jnp.mean(x * x, axis=-1, keepdims=True)
        xn_ref[...] = (x * lax.rsqrt(ms + EPS) * g_ref[...]).astype(BF16)

    z = lax.dot_general(xn_ref[...], w_ref[0], _NT, preferred_element_type=F32)
    first_sigmoid = pl.num_programs(1) - sigmoid_tiles
    first_plain = 0

    if onehot_seq is not None:
        first_plain = 1

        @pl.when(j == 0)
        def _():
            blk = _shr((i * tm + lax.broadcasted_iota(jnp.int32, (tm, 1), 0)) & (onehot_seq - 1), BLK)
            lane = lax.broadcasted_iota(jnp.int32, (1, tn), 1) & (LANES - 1)
            o_ref[...] = jnp.where(lane - HEAD_DIM == blk, 1.0, z).astype(o_ref.dtype)

    @pl.when((j >= first_plain) & (j < first_sigmoid))
    def _():
        o_ref[...] = z.astype(o_ref.dtype)

    @pl.when(j >= first_sigmoid)
    def _():
        o_ref[...] = jax.nn.sigmoid(z).astype(o_ref.dtype)


def _norm_matmul(x, g, w_t, layer, out_dtype, sigmoid_tiles=0, onehot_seq=None, rows=None,
                 tm=1024, tn=512):
    m, d = x.shape
    m = m if rows is None else rows
    n = w_t.shape[1]
    tm, tn = min(tm, m), min(tn, n)
    return pl.pallas_call(
        functools.partial(_norm_matmul_kernel, sigmoid_tiles=sigmoid_tiles, onehot_seq=onehot_seq),
        grid=(m // tm, n // tn),
        in_specs=[
            pl.BlockSpec((tm, d), lambda i, j: (i, 0)),
            pl.BlockSpec((1, d), lambda i, j: (0, 0)),
            pl.BlockSpec((1, tn, d), lambda i, j: (layer, j, 0)),
        ],
        out_specs=pl.BlockSpec((tm, tn), lambda i, j: (i, j)),
        out_shape=jax.ShapeDtypeStruct((m, n), out_dtype),
        scratch_shapes=[pltpu.VMEM((tm, d), BF16)],
        compiler_params=_params("parallel", "arbitrary"),
    )(x, g.reshape(1, d), w_t)


def _norm_matmul_t_kernel(x_ref, g_ref, w_ref, o_ref):
    x = x_ref[...]
    ms = jnp.mean(x * x, axis=-1, keepdims=True)
    xn = (x * lax.rsqrt(ms + EPS) * g_ref[...]).astype(BF16)
    o_ref[...] = lax.dot_general(w_ref[0], xn, _NT, preferred_element_type=F32).astype(o_ref.dtype)


def _norm_matmul_t(x, g, w_t, layer, out_dtype, rows, tm=1024):
    d = x.shape[1]
    n = w_t.shape[1]
    return pl.pallas_call(
        _norm_matmul_t_kernel,
        grid=(rows // tm,),
        in_specs=[
            pl.BlockSpec((tm, d), lambda i: (i, 0)),
            pl.BlockSpec((1, d), lambda i: (0, 0)),
            pl.BlockSpec((1, n, d), lambda i: (layer, 0, 0)),
        ],
        out_specs=pl.BlockSpec((n, tm), lambda i: (0, i)),
        out_shape=jax.ShapeDtypeStruct((n, rows), out_dtype),
        compiler_params=_params("parallel"),
    )(x, g.reshape(1, d), w_t)


def _compress_kernel(x_ref, pe_ref, w1_ref, w2_ref, o_ref, *, tb):
    acc = jnp.zeros((tb, 2 * LANES), F32)
    for r in range(BLK):
        xr = x_ref[pl.ds(r, tb, stride=BLK), :] + pe_ref[0, r:r + 1, :]
        acc = acc + jnp.dot(xr.astype(BF16), w1_ref[0, r], preferred_element_type=F32)
    hid = jax.nn.gelu(acc)
    o_ref[...] = jnp.dot(hid.astype(BF16), w2_ref[0], preferred_element_type=F32)


def _compress(rows, n_blocks, pe2, w1bd, w2bd, col0=0, tb=128):
    assert col0 % LANES == 0
    return pl.pallas_call(
        functools.partial(_compress_kernel, tb=tb),
        grid=(n_blocks // tb, 4),
        in_specs=[
            pl.BlockSpec((tb * BLK, LANES), lambda i, c: (i, col0 // LANES + c)),
            pl.BlockSpec((1, BLK, LANES), lambda i, c: (c // 2, 0, 0)),
            pl.BlockSpec((1, BLK, LANES, 2 * LANES), lambda i, c: (c // 2, 0, 0, 0)),
            pl.BlockSpec((1, 2 * LANES, LANES), lambda i, c: (c // 2, 0, 0)),
        ],
        out_specs=pl.BlockSpec((tb, LANES), lambda i, c: (i, c)),
        out_shape=jax.ShapeDtypeStruct((n_blocks, 4 * LANES), F32),
        compiler_params=_params("parallel", "arbitrary"),
    )(rows, pe2, w1bd, w2bd)


def _compress_weights(pe_k, w1_k, w2_k, pe_v, w1_v, w2_v):
    def one(pe, w1, w2):
        hid = w1.shape[1]
        w1r = w1.reshape(BLK, HEAD_DIM, hid)
        z = jnp.zeros_like(w1r)
        w1bd = jnp.concatenate(
            [jnp.concatenate([w1r, z], axis=2), jnp.concatenate([z, w1r], axis=2)], axis=1)
        z2 = jnp.zeros_like(w2)
        w2bd = jnp.concatenate(
            [jnp.concatenate([w2, z2], axis=1), jnp.concatenate([z2, w2], axis=1)], axis=0)
        return jnp.concatenate([pe, pe], axis=1), w1bd.astype(BF16), w2bd.astype(BF16)

    k, v = one(pe_k, w1_k, w2_k), one(pe_v, w1_v, w2_v)
    return tuple(jnp.stack([a, b]) for a, b in zip(k, v))


def _compress_pages_kernel(pt_ref, *refs, n_pages):
    del pt_ref
    pages = refs[:n_pages]
    pe_ref, w1_ref, w2_ref, o_ref = refs[n_pages:]
    per_page = 2 * N_KV
    rows = per_page * n_pages
    ridx = lax.broadcasted_iota(jnp.int32, (rows, 1), 0)
    is_v = (_shr(ridx, N_KV) & 1) == 1
    acc = jnp.zeros((rows, 2 * LANES), F32)
    for d in range(HEAD_DIM):
        a = jnp.concatenate([pg[0, pl.ds(d, per_page, stride=HEAD_DIM), :] for pg in pages], axis=0)
        ak = jnp.where(is_v, 0.0, a + pe_ref[0, d:d + 1, :])
        av = jnp.where(is_v, a + pe_ref[1, d:d + 1, :], 0.0)
        lhs = jnp.concatenate([ak, av], axis=1).astype(BF16)
        acc = acc + jnp.dot(lhs, w1_ref[d], preferred_element_type=F32)
    hid = jax.nn.gelu(acc)
    lhs = jnp.concatenate([jnp.where(is_v, 0.0, hid), jnp.where(is_v, hid, 0.0)], axis=1)
    o_ref[...] = jnp.dot(lhs.astype(BF16), w2_ref[...], preferred_element_type=F32)


def _compress_pages(cache_t, page_table, layer, n_phys, pe_t, w1p, w2p):
    nseq, n_pages = page_table.shape
    rows = 2 * N_KV * n_pages

    def page_map(p_i):
        return lambda b, pt: (layer * n_phys + pt[b * n_pages + p_i], 0, 0)

    in_specs = [pl.BlockSpec((1,) + cache_t.shape[1:], page_map(p_i)) for p_i in range(n_pages)]
    in_specs += [
        pl.BlockSpec(pe_t.shape, lambda b, pt: (0, 0, 0)),
        pl.BlockSpec(w1p.shape, lambda b, pt: (0, 0, 0)),
        pl.BlockSpec(w2p.shape, lambda b, pt: (0, 0)),
    ]
    grid_spec = pltpu.PrefetchScalarGridSpec(
        num_scalar_prefetch=1,
        grid=(nseq,),
        in_specs=in_specs,
        out_specs=pl.BlockSpec((rows, LANES), lambda b, pt: (b, 0)),
    )
    return pl.pallas_call(
        functools.partial(_compress_pages_kernel, n_pages=n_pages),
        grid_spec=grid_spec,
        out_shape=jax.ShapeDtypeStruct((nseq * rows, LANES), F32),
        compiler_params=_params("arbitrary"),
    )(page_table.reshape(-1), *([cache_t] * n_pages), pe_t, w1p, w2p)


def _compress_page_weights(pe_k, w1_k, w2_k, pe_v, w1_v, w2_v):
    eye = jnp.eye(2, dtype=F32)
    hid = w1_k.shape[1]

    def w1p(w1):
        w1d = w1.reshape(BLK, HEAD_DIM, hid).transpose(1, 0, 2)
        return jnp.einsum('drj,bc->dbrcj', w1d, eye)

    def w2p(w2):
        return jnp.einsum('je,bc->bjce', w2, eye)

    w1 = jnp.stack([w1p(w1_k), w1p(w1_v)], axis=1).reshape(HEAD_DIM, 4 * BLK, 2 * hid)
    w2 = jnp.stack([w2p(w2_k), w2p(w2_v)]).reshape(4 * hid, 2 * HEAD_DIM)
    pe_t = jnp.stack([jnp.tile(pe_k.T, (1, 2)), jnp.tile(pe_v.T, (1, 2))])
    return pe_t, w1.astype(BF16), w2.astype(BF16)


def _pool_kernel(u_ref, halo_ref, w_ref, s_ref, o_ref, *, tt, tiles_per_seq):
    i = pl.program_id(0)
    u = u_ref[...]
    halo = halo_ref[...]
    if tiles_per_seq is not None:
        halo = jnp.where(i % tiles_per_seq == 0, 0.0, halo)
        pos = (i % tiles_per_seq) * tt + lax.broadcasted_iota(jnp.int32, (tt, 1), 0)
    a = jnp.concatenate([halo, u], axis=0)
    gw = u.shape[1] // len(POOL_WINDOWS)
    for gi, w in enumerate(POOL_WINDOWS):
        b = a[:, gi * gw:(gi + 1) * gw]
        width = 1
        while width < w:
            b = b[:b.shape[0] - width] + b[width:]
            width *= 2
        lo = POOL_HALO + 1 - w
        win = b[lo:lo + tt]
        if tiles_per_seq is None:
            mean = win / float(w)
        else:
            mean = win / jnp.minimum(pos + 1, w).astype(F32)
        d = mean - u[:, gi * gw:(gi + 1) * gw]
        y = jnp.dot(d.astype(BF16), w_ref[gi], preferred_element_type=F32)
        o_ref[:, gi * gw:(gi + 1) * gw] = (y * s_ref[:, gi * gw:(gi + 1) * gw]).astype(o_ref.dtype)


def _pool_mix(u, m, w_grp, scale, tt, tiles_per_seq, col=0):
    width = scale.shape[0]
    hb = tt // POOL_HALO
    return pl.pallas_call(
        functools.partial(_pool_kernel, tt=tt, tiles_per_seq=tiles_per_seq),
        grid=(m // tt,),
        in_specs=[
            pl.BlockSpec((tt, width), lambda i: (i, col)),
            pl.BlockSpec((POOL_HALO, width), lambda i: (jnp.maximum(i * hb - 1, 0), col)),
            pl.BlockSpec(w_grp.shape, lambda i: (0, 0, 0)),
            pl.BlockSpec((1, width), lambda i: (0, 0)),
        ],
        out_specs=pl.BlockSpec((tt, width), lambda i: (i, 0)),
        out_shape=jax.ShapeDtypeStruct((m, width), BF16),
        compiler_params=_params("parallel"),
    )(u, u, w_grp, scale.reshape(1, width))


def _gated_branch_kernel(a_ref, b_ref, wa_ref, wb_ref, x_ref, g_ref, wga_ref, wgb_ref, o_ref, xn_ref):
    @pl.when(pl.program_id(1) == 0)
    def _():
        x = x_ref[...]
        ms = jnp.mean(x * x, axis=-1, keepdims=True)
        xn_ref[...] = (x * lax.rsqrt(ms + EPS) * g_ref[...]).astype(BF16)

    xn = xn_ref[...]
    ga = jax.nn.sigmoid(lax.dot_general(xn, wga_ref[0], _NT, preferred_element_type=F32))
    gb = jax.nn.sigmoid(lax.dot_general(xn, wgb_ref[0], _NT, preferred_element_type=F32))
    a = jnp.dot(a_ref[...], wa_ref[0], preferred_element_type=F32)
    b = jnp.dot(b_ref[...], wb_ref[0], preferred_element_type=F32)
    o_ref[...] = (ga * a + gb * b).astype(o_ref.dtype)


def _gated_branches(a, b, wa, wb, x, g, wg_t, layer, tm=1024, tn=512):
    m, k = a.shape
    d = x.shape[1]
    n = wa.shape[2]
    nj = n // tn
    return pl.pallas_call(
        _gated_branch_kernel,
        grid=(m // tm, nj),
        in_specs=[
            pl.BlockSpec((tm, k), lambda i, j: (i, 0)),
            pl.BlockSpec((tm, k), lambda i, j: (i, 0)),
            pl.BlockSpec((1, k, tn), lambda i, j: (layer, 0, j)),
            pl.BlockSpec((1, k, tn), lambda i, j: (layer, 0, j)),
            pl.BlockSpec((tm, d), lambda i, j: (i, 0)),
            pl.BlockSpec((1, d), lambda i, j: (0, 0)),
            pl.BlockSpec((1, tn, d), lambda i, j: (layer, j, 0)),
            pl.BlockSpec((1, tn, d), lambda i, j: (layer, j + nj, 0)),
        ],
        out_specs=pl.BlockSpec((tm, tn), lambda i, j: (i, j)),
        out_shape=jax.ShapeDtypeStruct((m, n), BF16),
        scratch_shapes=[pltpu.VMEM((tm, d), BF16)],
        compiler_params=_params("parallel", "arbitrary"),
    )(a, b, wa, wb, x, g.reshape(1, d), wg_t, wg_t)


def _resid_matmul_kernel(x_ref, a_ref, w_ref, o_ref):
    o_ref[...] = x_ref[...] + jnp.dot(a_ref[...], w_ref[0], preferred_element_type=F32)


def _resid_matmul(x, a, w, layer, tm=1024, tn=512):
    m, k = a.shape
    n = w.shape[2]
    return pl.pallas_call(
        _resid_matmul_kernel,
        grid=(m // tm, n // tn),
        in_specs=[
            pl.BlockSpec((tm, tn), lambda i, j: (i, j)),
            pl.BlockSpec((tm, k), lambda i, j: (i, 0)),
            pl.BlockSpec((1, k, tn), lambda i, j: (layer, 0, j)),
        ],
        out_specs=pl.BlockSpec((tm, tn), lambda i, j: (i, j)),
        out_shape=jax.ShapeDtypeStruct((m, n), F32),
        compiler_params=_params("parallel", "arbitrary"),
    )(x, a, w)


def _mlp_kernel(x_ref, g_ref, wu_ref, wd_ref, gf_ref, o_ref, h_ref, *, final_norm):
    f = pl.program_id(1)

    @pl.when(f == 0)
    def _():
        x = x_ref[...]
        ms = jnp.mean(x * x, axis=-1, keepdims=True)
        h_ref[...] = (x * lax.rsqrt(ms + EPS) * g_ref[...]).astype(BF16)
        o_ref[...] = x

    up = jnp.dot(h_ref[...], wu_ref[0], preferred_element_type=F32)
    act = jnp.square(jnp.maximum(up, 0.0)).astype(BF16)
    o_ref[...] += jnp.dot(act, wd_ref[0], preferred_element_type=F32)

    if final_norm:
        @pl.when(f == pl.num_programs(1) - 1)
        def _():
            y = o_ref[...]
            ms = jnp.mean(y * y, axis=-1, keepdims=True)
            o_ref[...] = y * lax.rsqrt(ms + EPS) * gf_ref[...]


def _mlp(x, g, w_up, w_down, g_final, final_norm, layer, tm=1024, tf=512):
    m, d = x.shape
    dff = w_up.shape[2]
    return pl.pallas_call(
        functools.partial(_mlp_kernel, final_norm=final_norm),
        grid=(m // tm, dff // tf),
        in_specs=[
            pl.BlockSpec((tm, d), lambda i, f: (i, 0)),
            pl.BlockSpec((1, d), lambda i, f: (0, 0)),
            pl.BlockSpec((1, d, tf), lambda i, f: (layer, 0, f)),
            pl.BlockSpec((1, tf, d), lambda i, f: (layer, f, 0)),
            pl.BlockSpec((1, d), lambda i, f: (0, 0)),
        ],
        out_specs=pl.BlockSpec((tm, d), lambda i, f: (i, 0)),
        out_shape=jax.ShapeDtypeStruct((m, d), F32),
        scratch_shapes=[pltpu.VMEM((tm, d), BF16)],
        compiler_params=_params("parallel", "arbitrary"),
    )(x, g.reshape(1, d), w_up, w_down, g_final.reshape(1, d))


def _shr(x, pow2):
    assert pow2 & (pow2 - 1) == 0
    return lax.shift_right_logical(x, pow2.bit_length() - 1)


def _masked_softmax(s, mask):
    sm = jnp.where(mask, s, NEG)
    e = jnp.exp(sm - jnp.max(sm, axis=-1, keepdims=True))
    return e / jnp.sum(e, axis=-1, keepdims=True)


def _select_blocks(imp, qpos, n_loop):
    r, nb = imp.shape
    bidx = lax.broadcasted_iota(jnp.int32, (r, nb), 1)
    cur = _shr(qpos, BLK)
    forced = (bidx == 0) | (bidx == cur) | (bidx == cur - 1)
    cand = bidx * BLK <= qpos
    score = jnp.where(cand, jnp.where(forced, FORCED, imp), NEG)
    rank = jnp.zeros((r, nb), F32)
    for j in range(n_loop):
        c = score[:, j:j + 1]
        beats = (c > score) | ((c == score) & (bidx > j))
        rank = rank + jnp.where(beats, 1.0, 0.0)
    return jnp.where(cand & (rank < TOP_N), 1.0, 0.0)


def _select_blocks_t(imp_t, qpos_t):
    nb, r = imp_t.shape
    bidx = lax.broadcasted_iota(jnp.int32, (nb, r), 0)
    cur = _shr(qpos_t, BLK)
    forced = (bidx == 0) | (bidx == cur) | (bidx == cur - 1)
    cand = bidx * BLK <= qpos_t
    score = jnp.where(cand, jnp.where(forced, FORCED, imp_t), NEG)
    n_chunks = nb // SUBLANES
    chunks = [score[SUBLANES * c:SUBLANES * (c + 1)] for c in range(n_chunks)]
    ranks = [jnp.zeros((SUBLANES, r), F32) for _ in range(n_chunks)]
    sub = lax.broadcasted_iota(jnp.int32, (SUBLANES, r), 0)
    for j in range(nb):
        cj = score[j:j + 1, :]
        for c in range(n_chunks):
            lo = SUBLANES * c
            ge = jnp.where(cj >= chunks[c], 1.0, 0.0)
            gt = jnp.where(cj > chunks[c], 1.0, 0.0)
            if lo > j:
                term = ge
            elif lo + SUBLANES - 1 < j:
                term = gt
            else:
                term = jnp.where(sub > j - lo, ge, gt)
            ranks[c] = ranks[c] + term
    rank = jnp.concatenate(ranks, axis=0)
    return jnp.where(cand & (rank < TOP_N), 1.0, 0.0)


def _block_expand(n_blocks, k0, n_keys):
    b = lax.broadcasted_iota(jnp.int32, (n_blocks, n_keys), 0)
    k = lax.broadcasted_iota(jnp.int32, (n_blocks, n_keys), 1) + k0
    return jnp.where(_shr(k, BLK) == b, 1.0, 0.0).astype(BF16)


def _softmax_keys(scores, mask):
    sm = jnp.where(mask, scores, NEG)
    e = jnp.exp(sm - jnp.max(sm, axis=0, keepdims=True))
    return e / jnp.sum(e, axis=0, keepdims=True)


def _nsa_prompt_kernel(q_ref, kc_ref, vc_ref, ks_ref, vs_ref, kw_ref, vw_ref, g_ref,
                       o_ref, *, seq, heads):
    i = pl.program_id(2)
    rows = GROUP * Q_TILE
    hw = GROUP * HEAD_DIM
    nb = seq // BLK
    qpos = i * Q_TILE + (lax.broadcasted_iota(jnp.int32, (1, rows), 1) & (Q_TILE - 1))
    bcol = lax.broadcasted_iota(jnp.int32, (nb, 1), 0)
    bidx = lax.broadcasted_iota(jnp.int32, (1, nb), 1)
    cmask = bcol * BLK + (BLK - 1) <= qpos
    own = _shr(bidx, Q_TILE // BLK) == i

    def prepare(h):
        qt = q_ref[:, h * hw:(h + 1) * hw].astype(BF16)
        q = jnp.concatenate([qt[:, g * HEAD_DIM:(g + 1) * HEAD_DIM] for g in range(GROUP)], axis=0)
        s_c = lax.dot_general(kc_ref[0, h], q, _NT, preferred_element_type=F32)
        p_c = jnp.where(cmask, _softmax_keys(s_c, cmask), 0.0)
        o_cmp = jnp.dot(vc_ref[0, h], p_c.astype(BF16), preferred_element_type=F32)
        imp = p_c[:, 0:Q_TILE]
        for g in range(1, GROUP):
            imp = imp + p_c[:, g * Q_TILE:(g + 1) * Q_TILE]
        sel = _select_blocks_t(imp, qpos[:, 0:Q_TILE]).T
        block_bias = jnp.where((sel > 0.5) & jnp.logical_not(own), 0.0, NEG).astype(BF16)
        if nb < HEAD_DIM:
            block_bias = jnp.concatenate([block_bias, jnp.zeros((Q_TILE, HEAD_DIM - nb), BF16)], axis=1)
        q_aug = jnp.concatenate([q, jnp.concatenate([block_bias] * GROUP, axis=0)], axis=1)
        q_pad = jnp.concatenate([q, jnp.zeros((rows, HEAD_DIM), BF16)], axis=1)
        return q_pad, q_aug, o_cmp

    def online_step(carry, scores, v_t):
        m, l, acc = carry
        m_new = jnp.maximum(m, jnp.max(scores, axis=0, keepdims=True))
        alpha = jnp.exp(m - m_new)
        pj = jnp.exp(scores - m_new)
        l = alpha * l + jnp.sum(pj, axis=0, keepdims=True)
        acc = alpha * acc + jnp.dot(v_t, pj.astype(BF16), preferred_element_type=F32)
        return m_new, l, acc

    prepared = [prepare(h) for h in range(heads)]

    def body(j, carries):
        k0 = pl.multiple_of(j * KEY_TILE, KEY_TILE)
        scores = [lax.dot_general(ks_ref[pl.ds(k0, KEY_TILE), h * LANES:(h + 1) * LANES],
                                  prepared[h][1], _NT, preferred_element_type=F32)
                  for h in range(heads)]
        return tuple(online_step(carries[h], scores[h],
                                 vs_ref[h * HEAD_DIM:(h + 1) * HEAD_DIM, pl.ds(k0, KEY_TILE)])
                     for h in range(heads))

    n_tiles = lax.div(i * Q_TILE + KEY_TILE - 1, KEY_TILE)
    init = tuple((jnp.full((1, rows), NEG, F32), jnp.zeros((1, rows), F32),
                  jnp.zeros((HEAD_DIM, rows), F32)) for _ in range(heads))
    carries = lax.fori_loop(0, n_tiles, body, init)

    d0 = pl.multiple_of(i * Q_TILE, Q_TILE)
    causal = d0 + lax.broadcasted_iota(jnp.int32, (Q_TILE, 1), 0) <= qpos
    span = WINDOW + Q_TILE
    start = pl.multiple_of(jnp.maximum(i * Q_TILE - WINDOW, 0), Q_TILE)
    wpos = start + lax.broadcasted_iota(jnp.int32, (span, 1), 0)
    wmask = (wpos <= qpos) & (wpos >= qpos - WINDOW)
    sd = [lax.dot_general(ks_ref[pl.ds(d0, Q_TILE), h * LANES:(h + 1) * LANES], prepared[h][0],
                          _NT, preferred_element_type=F32) for h in range(heads)]
    sw = [lax.dot_general(kw_ref[pl.ds(start, span), h * LANES:(h + 1) * LANES], prepared[h][0],
                          _NT, preferred_element_type=F32) for h in range(heads)]
    for h in range(heads):
        o_cmp = prepared[h][2]
        _, l, acc = online_step(carries[h], jnp.where(causal, sd[h], NEG),
                                vs_ref[h * HEAD_DIM:(h + 1) * HEAD_DIM, pl.ds(d0, Q_TILE)])
        o_sel = acc / l
        vw_t = vw_ref[h * HEAD_DIM:(h + 1) * HEAD_DIM, pl.ds(start, span)]
        o_win = jnp.dot(vw_t, _softmax_keys(sw[h], wmask).astype(BF16), preferred_element_type=F32)

        g_t = g_ref[:, h * LANES:(h + 1) * LANES].T
        for g in range(GROUP):
            sl = slice(g * Q_TILE, (g + 1) * Q_TILE)
            o_g = (g_t[g:g + 1] * o_cmp[:, sl] + g_t[GROUP + g:GROUP + g + 1] * o_sel[:, sl]
                   + g_t[2 * GROUP + g:2 * GROUP + g + 1] * o_win[:, sl])
            c0 = h * hw + g * HEAD_DIM
            o_ref[:, c0:c0 + HEAD_DIM] = o_g.T.astype(o_ref.dtype)


def _nsa_prompt(q, q_col, kc, vc, keys, vals_t, gates, gate_col, n, seq, heads=4):
    nq = seq // Q_TILE
    nb = seq // BLK
    assert nb <= HEAD_DIM
    row = lambda b, h, i: (b * nq + i, h)
    head = lambda b, h, i: (b, h, 0, 0)
    steps = N_KV // heads
    return pl.pallas_call(
        functools.partial(_nsa_prompt_kernel, seq=seq, heads=heads),
        grid=(n, steps, nq),
        in_specs=[
            pl.BlockSpec((Q_TILE, heads * GROUP * HEAD_DIM), lambda b, h, i: (b * nq + i, q_col * steps + h)),
            pl.BlockSpec((1, heads, nb, HEAD_DIM), head), pl.BlockSpec((1, heads, HEAD_DIM, nb), head),
            pl.BlockSpec((seq, heads * LANES), lambda b, h, i: (b, h)),
            pl.BlockSpec((heads * HEAD_DIM, seq), lambda b, h, i: (h, b)),
            pl.BlockSpec((seq, heads * LANES), lambda b, h, i: (b, steps + h)),
            pl.BlockSpec((heads * HEAD_DIM, seq), lambda b, h, i: (steps + h, b)),
            pl.BlockSpec((Q_TILE, heads * LANES), lambda b, h, i: (b * nq + i, gate_col * steps + h)),
        ],
        out_specs=pl.BlockSpec((Q_TILE, heads * GROUP * HEAD_DIM), row),
        out_shape=jax.ShapeDtypeStruct((n * seq, N_KV * GROUP * HEAD_DIM), BF16),
        compiler_params=_params("parallel", "parallel", "arbitrary"),
    )(q, kc, vc, keys, vals_t, keys, vals_t, gates)


def _nsa_sample_kernel(pt_ref, q_ref, g_ref, *refs, n_pages, page, dec, wb):
    del pt_ref
    cmp_seq, cmp_new = refs[:2]
    sel_pages = refs[2:n_pages + 2]
    sel_new = refs[n_pages + 2]
    win_ref, win_new, _, o_ref, wo_ref, kc_ref, k_ref, v_ref, kw_ref, vw_ref = refs[n_pages + 3:]
    past = n_pages * page
    rows = N_KV * GROUP * dec
    q = q_ref[0]
    ridx = lax.broadcasted_iota(jnp.int32, (rows, 1), 0)
    qpos = past + (ridx & (dec - 1))

    n_keys = past + page
    for p_i in range(n_pages + 1):
        pg = sel_pages[p_i][0] if p_i < n_pages else sel_new[0]
        k_ref[:, p_i * page:(p_i + 1) * page] = pg[:KV_W].astype(BF16)
        v_ref[:, p_i * page:(p_i + 1) * page] = pg[KV_W:].astype(BF16)
    wn = win_ref[0]
    nw = win_new[0]
    kw_ref[:, 0:wb] = wn[:KV_W].astype(BF16)
    vw_ref[:, 0:wb] = wn[KV_W:].astype(BF16)
    kw_ref[:, wb:wb + page] = nw[:KV_W].astype(BF16)
    vw_ref[:, wb:wb + page] = nw[KV_W:].astype(BF16)

    bpp = page // BLK
    kc_ref[...] = jnp.zeros_like(kc_ref)
    kc_ref[0:bpp * n_pages, :] = cmp_seq[0]
    kc_ref[bpp * n_pages:bpp * n_pages + 1, :] = cmp_new[0, 0:1, :]
    nbp = kc_ref.shape[0]
    s = lax.dot_general(q, kc_ref[:, :KV_W].astype(BF16), _NT, preferred_element_type=F32)
    bidx = lax.broadcasted_iota(jnp.int32, (1, nbp), 1)
    cmask = bidx * BLK + (BLK - 1) <= qpos
    p = jnp.where(cmask, _masked_softmax(s, cmask), 0.0)
    o_cmp = jnp.dot(p.astype(BF16), kc_ref[:, KV_W:].astype(BF16), preferred_element_type=F32)

    imps = []
    for h in range(N_KV):
        base = h * GROUP * dec
        acc = p[base:base + dec]
        for g in range(1, GROUP):
            acc = acc + p[base + g * dec:base + (g + 1) * dec]
        imps.append(acc)
    imp = jnp.concatenate(imps, axis=0)
    qpos_ht = past + (lax.broadcasted_iota(jnp.int32, (N_KV * dec, 1), 0) & (dec - 1))
    sel = _select_blocks(imp, qpos_ht, bpp * n_pages + 1)
    sel_rows = jnp.concatenate(
        [sel[h * dec:(h + 1) * dec] for h in range(N_KV) for _ in range(GROUP)], axis=0)

    sw = jnp.dot(q, kw_ref[...], preferred_element_type=F32)
    wpos = past - wb + lax.broadcasted_iota(jnp.int32, (1, wb + page), 1)
    pw = _masked_softmax(sw, (wpos <= qpos) & (wpos >= qpos - WINDOW))
    o_win = lax.dot_general(pw.astype(BF16), vw_ref[...], _NT, preferred_element_type=F32)

    ss = jnp.dot(q, k_ref[...], preferred_element_type=F32)
    key_sel = jnp.dot(sel_rows.astype(BF16), _block_expand(nbp, 0, n_keys),
                      preferred_element_type=F32)
    kpos = lax.broadcasted_iota(jnp.int32, (1, n_keys), 1)
    ps = _masked_softmax(ss, (key_sel > 0.5) & (kpos <= qpos))
    o_sel = lax.dot_general(ps.astype(BF16), v_ref[...], _NT, preferred_element_type=F32)

    gt = g_ref[0]
    o = gt[:, 0:1] * o_cmp + gt[:, 1:2] * o_sel + gt[:, 2:3] * o_win
    col = lax.broadcasted_iota(jnp.int32, (1, KV_W), 1)
    o = jnp.where(_shr(ridx, GROUP * dec) == _shr(col, HEAD_DIM), o, 0.0)
    hr = GROUP * dec
    out = o[0:hr]
    for h in range(1, N_KV):
        out = out + o[h * hr:(h + 1) * hr]
    o_ref[0] = out

    shifted = pltpu.roll(wn, wb - dec, 1)
    tail = pltpu.roll(nw, page - dec, 1)
    lane = lax.broadcasted_iota(jnp.int32, (1, page), 1)
    wo_ref[0, :, 0:wb - page] = shifted[:, 0:wb - page]
    wo_ref[0, :, wb - page:wb] = jnp.where(lane >= page - dec, tail, shifted[:, wb - page:wb])


def _nsa_sample(page_table, layer, n_phys, qbd, gates, cmp_seq, cmp_new, sel_cache_t, sel_new,
                win_cache_t, win_new, win_out):
    nseq, n_pages = page_table.shape
    page = sel_cache_t.shape[2]
    rows = qbd.shape[1]
    dec = rows // (N_KV * GROUP)
    wb = win_cache_t.shape[2]
    bpp = page // BLK
    nbp = LANES
    assert bpp * n_pages + 1 <= nbp and wb == WINDOW and page == LANES and dec <= SUBLANES

    def page_map(p_i, base):
        return lambda b, pt: (base + pt[b * n_pages + p_i], 0, 0)

    seq_map = lambda b, pt: (b, 0, 0)
    in_specs = [
        pl.BlockSpec((1, rows, KV_W), seq_map),
        pl.BlockSpec((1, rows, 8), seq_map),
    ]
    in_specs += [pl.BlockSpec((1, bpp * n_pages, 2 * KV_W), seq_map),
                 pl.BlockSpec((1, 8, 2 * KV_W), seq_map)]
    in_specs += [pl.BlockSpec((1, 2 * KV_W, page), page_map(p_i, layer * n_phys))
                 for p_i in range(n_pages)]
    in_specs += [
        pl.BlockSpec((1, 2 * KV_W, page), seq_map),
        pl.BlockSpec((1, 2 * KV_W, wb), lambda b, pt: (layer * nseq + b, 0, 0)),
        pl.BlockSpec((1, 2 * KV_W, page), seq_map),
        pl.BlockSpec(memory_space=pl.ANY),
    ]
    layer_seq_map = lambda b, pt: (layer * nseq + b, 0, 0)
    n_keys = (n_pages + 1) * page
    grid_spec = pltpu.PrefetchScalarGridSpec(
        num_scalar_prefetch=1,
        grid=(nseq,),
        in_specs=in_specs,
        out_specs=[
            pl.BlockSpec((1, GROUP * dec, KV_W), seq_map),
            pl.BlockSpec((1, 2 * KV_W, wb), layer_seq_map),
        ],
        scratch_shapes=[
            pltpu.VMEM((nbp, 2 * KV_W), F32),
            pltpu.VMEM((KV_W, n_keys), BF16),
            pltpu.VMEM((KV_W, n_keys), BF16),
            pltpu.VMEM((KV_W, wb + page), BF16),
            pltpu.VMEM((KV_W, wb + page), BF16),
        ],
    )
    args = (page_table.reshape(-1), qbd, gates, cmp_seq, cmp_new,
            *([sel_cache_t] * n_pages), sel_new, win_cache_t, win_new, win_out)
    return pl.pallas_call(
        functools.partial(_nsa_sample_kernel, n_pages=n_pages, page=page, dec=dec, wb=wb),
        grid_spec=grid_spec,
        out_shape=[
            jax.ShapeDtypeStruct((nseq, GROUP * dec, KV_W), F32),
            jax.ShapeDtypeStruct(win_out.shape, F32),
        ],
        input_output_aliases={len(args) - 1: 1},
        compiler_params=_params("arbitrary"),
    )(*args)


def _heads_major(a, n, seq):
    return a.reshape(n, seq, N_KV, HEAD_DIM).transpose(0, 2, 1, 3).astype(BF16)


def _heads_major_t(a, n, seq):
    return a.reshape(n, seq, N_KV, HEAD_DIM).transpose(0, 2, 3, 1).astype(BF16)


def _pad_rows(a, rows):
    return jnp.pad(a, ((0, 0), (0, rows - a.shape[1]), (0, 0)))


def _channel_major(cache):
    depth, a, rows = cache.shape[:3]
    return cache.transpose(0, 1, 3, 4, 5, 2).reshape(depth * a, 2 * KV_W, rows)


def kernel(x_prompt, x_sample, cache_cmp_kv, cache_sel_kv, cache_win_kv, state_pool, page_table, g_mix, w_in, pe_ck, w_ck1, w_ck2, pe_cv, w_cv1, w_cv2, w_pool_grp, pool_scale, w_br_nsa, w_br_pool, w_o, g_mlp, w_up, w_down, g_final):
    n, seq, d = x_prompt.shape
    nseq, dec, _ = x_sample.shape
    depth = w_in.shape[0]
    n_phys, page = cache_cmp_kv.shape[1], cache_cmp_kv.shape[2]
    wb = cache_win_kv.shape[2]
    mp, ms = n * seq, nseq * dec
    nsa_w = N_KV * GROUP * HEAD_DIM
    kv_cols = 2 * KV_W
    n_gate = 3 * N_KV * GROUP
    pool_w = state_pool.shape[-1]
    pool_state = state_pool.shape[2]
    c_q, c_kv, c_ng = nsa_w, nsa_w + 3 * kv_cols, nsa_w + 3 * kv_cols + n_gate
    c_u = c_ng + pool_w

    x = jnp.concatenate([x_prompt.reshape(mp, d), x_sample.reshape(ms, d)], axis=0)
    kv_shape = (2, N_KV, HEAD_DIM)
    outs = [[] for _ in range(8)]
    cmp_cache_t = _channel_major(cache_cmp_kv)
    sel_cache_t = _channel_major(cache_sel_kv)
    win_cache_t = _channel_major(cache_win_kv)
    win_out = jnp.zeros(win_cache_t.shape, F32)

    w_t = w_in.transpose(0, 2, 1)
    w_q = (w_t[:, :c_q] * HEAD_DIM ** -0.5).astype(BF16)
    w_kv = w_t[:, c_q:c_kv].astype(BF16)
    w_ng = w_t[:, c_kv:c_ng].reshape(depth, 3, N_KV, GROUP, d).transpose(0, 2, 1, 3, 4)
    w_ng = jnp.pad(w_ng.reshape(depth, N_KV, 3 * GROUP, d), ((0, 0), (0, 0), (0, LANES - 3 * GROUP), (0, 0)))
    w_ng = w_ng.reshape(depth, N_KV * LANES, d).astype(BF16)
    w_u = w_t[:, c_ng:c_u].astype(BF16)
    w_mg = w_t[:, c_u:].astype(BF16)
    w_cat = jnp.concatenate([w_q, w_u, w_kv, w_ng], axis=1)
    o_u, o_kv, o_ng = nsa_w, nsa_w + pool_w, nsa_w + pool_w + 3 * kv_cols
    assert nsa_w == N_KV * GROUP * HEAD_DIM and o_u % pool_w == 0 and o_ng % (N_KV * LANES) == 0
    def key_slots(rows_t):
        r = rows_t.reshape(depth, N_KV, HEAD_DIM, d)
        return jnp.pad(r, ((0, 0), (0, 0), (0, LANES - HEAD_DIM), (0, 0))).reshape(depth, N_KV * LANES, d)

    c_sel, c_win = c_q + kv_cols, c_q + 2 * kv_cols
    w_keys = jnp.concatenate([key_slots(w_t[:, c_sel:c_sel + KV_W]),
                              key_slots(w_t[:, c_win:c_win + KV_W])], axis=1).astype(BF16)
    w_vals = jnp.concatenate([w_t[:, c_sel + KV_W:c_sel + kv_cols],
                              w_t[:, c_win + KV_W:c_win + kv_cols]], axis=1).astype(BF16)
    w_bn, w_bp, w_ob = w_br_nsa.astype(BF16), w_br_pool.astype(BF16), w_o.astype(BF16)
    w_upb, w_downb = w_up.astype(BF16), w_down.astype(BF16)

    for l in range(depth):
        z = _norm_matmul(x, g_mix[l], w_cat, l, F32, sigmoid_tiles=1)
        zkv = z[:, o_kv:o_ng]
        q = z[:, :nsa_w]
        ng = z[:, o_ng:]

        cmp_w = (pe_ck[l], w_ck1[l], w_ck2[l], pe_cv[l], w_cv1[l], w_cv2[l])
        cw = _compress_weights(*cmp_w)

        cmp_p = _compress(z, mp // BLK, *cw, col0=o_kv)
        kc_p = _heads_major(cmp_p[:, :KV_W], n, seq // BLK)
        vc_p = _heads_major_t(cmp_p[:, KV_W:], n, seq // BLK)
        zp = zkv[:mp]
        keys_p = _norm_matmul(x, g_mix[l], w_keys, l, BF16, onehot_seq=seq, rows=mp)
        vals_p = _norm_matmul_t(x, g_mix[l], w_vals, l, BF16, mp)
        o_nsa_p = _nsa_prompt(z, 0, kc_p, vc_p, keys_p, vals_p, z, o_ng // (N_KV * LANES), n, seq)

        zs = zkv[mp:].reshape(nseq, dec, 3 * kv_cols)
        new_c = _pad_rows(zs[:, :, :kv_cols], page)
        lanes_t = lambda a: jnp.pad(a.transpose(0, 2, 1), ((0, 0), (0, 0), (0, page - dec)))
        new_s = lanes_t(zs[:, :, kv_cols:2 * kv_cols])
        new_w = lanes_t(zs[:, :, 2 * kv_cols:])
        bpp = page // BLK
        cmp_seq = _compress_pages(cmp_cache_t, page_table, l, n_phys, *_compress_page_weights(*cmp_w))
        n_pages = page_table.shape[1]
        cmp_seq = cmp_seq.reshape(nseq, n_pages, 2, N_KV, bpp, HEAD_DIM).transpose(0, 1, 4, 2, 3, 5)
        cmp_seq = cmp_seq.reshape(nseq, n_pages * bpp, kv_cols)
        cmp_new = _compress(new_c.reshape(nseq * page, kv_cols), nseq * bpp, *cw)
        cmp_new = _pad_rows(cmp_new.reshape(nseq, bpp, kv_cols)[:, :1], 8)

        qs = q[mp:].astype(BF16).reshape(nseq, dec, N_KV, GROUP, HEAD_DIM)
        eye = jnp.eye(N_KV, dtype=BF16)
        qbd = jnp.einsum('bthgd,hk->bhgtkd', qs, eye).reshape(nseq, N_KV * GROUP * dec, KV_W)
        gs = ng[mp:].reshape(nseq, dec, N_KV, LANES)[..., :3 * GROUP].reshape(nseq, dec, N_KV, 3, GROUP)
        gs = gs.transpose(0, 2, 4, 1, 3).reshape(nseq, N_KV * GROUP * dec, 3)
        gs = jnp.pad(gs, ((0, 0), (0, 0), (0, 5)))
        o_s, win_out = _nsa_sample(page_table, l, n_phys, qbd, gs, cmp_seq, cmp_new, sel_cache_t, new_s,
                                   win_cache_t, new_w, win_out)
        o_nsa_s = o_s.reshape(nseq, GROUP, dec, N_KV, HEAD_DIM).transpose(0, 2, 3, 1, 4)
        o_nsa = jnp.concatenate([o_nsa_p, o_nsa_s.reshape(ms, nsa_w).astype(BF16)], axis=0)

        w_grp = w_pool_grp[l].astype(BF16)
        u = z[:, o_u:o_kv]
        u_p, u_s = u[:mp], u[mp:].reshape(nseq, dec, pool_w)
        o_pool_p = _pool_mix(z, mp, w_grp, pool_scale[l], 512, seq // 512, col=o_u // pool_w)
        hist = jnp.pad(state_pool[l], ((0, 0), (POOL_HALO - pool_state, 0), (0, 0)))
        xx_s = jnp.concatenate([hist, u_s], axis=1)
        rows_s = POOL_HALO + dec
        o_pool_s = _pool_mix(xx_s.reshape(nseq * rows_s, pool_w), nseq * rows_s, w_grp, pool_scale[l],
                             16 * rows_s, None)
        o_pool_s = o_pool_s.reshape(nseq, rows_s, pool_w)[:, POOL_HALO:].reshape(ms, pool_w)
        o_pool = jnp.concatenate([o_pool_p, o_pool_s], axis=0)

        m = _gated_branches(o_nsa, o_pool, w_bn, w_bp, x, g_mix[l], w_mg, l)
        x1 = _resid_matmul(x, m, w_ob, l)
        x = _mlp(x1, g_mlp[l], w_upb, w_downb, g_final, l == depth - 1, l)

        zp6 = zp.reshape(n, seq, 3, *kv_shape)
        outs[0].append(zp6[:, :, 0])
        outs[1].append(zp6[:, :, 1])
        outs[2].append(zp6[:, -min(WINDOW, seq):, 2])
        outs[3].append(u_p.reshape(n, seq, pool_w)[:, -pool_state:])
        zs6 = zs.reshape(nseq, dec, 3, *kv_shape)
        outs[4].append(zs6[:, :, 0])
        outs[5].append(zs6[:, :, 1])
        outs[7].append(xx_s[:, -pool_state:])

    y_prompt = x[:mp].reshape(n, seq, d)
    y_sample = x[mp:].reshape(nseq, dec, d)
    s_win = win_out.reshape(depth, nseq, *kv_shape, wb).transpose(0, 1, 5, 2, 3, 4)
    stacked = [s_win if i == 6 else jnp.stack(o) for i, o in enumerate(outs)]
    return (y_prompt, y_sample) + tuple(stacked)
```

```python
import functools

import jax
import jax.numpy as jnp
from jax import lax
from jax.experimental import pallas as pl
from jax.experimental.pallas import tpu as pltpu

F32 = jnp.float32
BF16 = jnp.bfloat16

N_KV = 4
GROUP = 4
HEAD_DIM = 64
KV_W = N_KV * HEAD_DIM
BLK = 64
TOP_N = 16
WINDOW = 512
Q_TILE = 128
KEY_TILE = 512
POOL_WINDOWS = (2, 4, 8, 16)
POOL_HALO = 16
EPS = 1e-6
NEG = -1e30
FORCED = 1e4
LANES = 128
SUBLANES = 8
VMEM_LIMIT = 56 * 1024 * 1024

_NT = (((1,), (1,)), ((), ()))


def _params(*sem):
    return pltpu.CompilerParams(dimension_semantics=sem, vmem_limit_bytes=VMEM_LIMIT)


def _norm_matmul_kernel(x_ref, g_ref, w_ref, o_ref, xn_ref, *, sigmoid_tiles, onehot_seq):
    i, j = pl.program_id(0), pl.program_id(1)
    tm, tn = o_ref.shape

    @pl.when(j == 0)
    def _():
        x = x_ref[...]
        ms = jnp.mean(x * x, axis=-1, keepdims=True)
        xn_ref[...] = (x * lax.rsqrt(ms + EPS) * g_ref[...]).astype(BF16)

    z = lax.dot_general(xn_ref[...], w_ref[0], _NT, preferred_element_type=F32)
    first_sigmoid = pl.num_programs(1) - sigmoid_tiles
    first_plain = 0

    if onehot_seq is not None:
        first_plain = 1

        @pl.when(j == 0)
        def _():
            blk = _shr((i * tm + lax.broadcasted_iota(jnp.int32, (tm, 1), 0)) & (onehot_seq - 1), BLK)
            lane = lax.broadcasted_iota(jnp.int32, (1, tn), 1) & (LANES - 1)
            o_ref[...] = jnp.where(lane - HEAD_DIM == blk, 1.0, z).astype(o_ref.dtype)

    @pl.when((j >= first_plain) & (j < first_sigmoid))
    def _():
        o_ref[...] = z.astype(o_ref.dtype)

    @pl.when(j >= first_sigmoid)
    def _():
        o_ref[...] = jax.nn.sigmoid(z).astype(o_ref.dtype)


def _norm_matmul(x, g, w_t, layer, out_dtype, sigmoid_tiles=0, onehot_seq=None, rows=None,
                 tm=1024, tn=512):
    m, d = x.shape
    m = m if rows is None else rows
    n = w_t.shape[1]
    tm, tn = min(tm, m), min(tn, n)
    return pl.pallas_call(
        functools.partial(_norm_matmul_kernel, sigmoid_tiles=sigmoid_tiles, onehot_seq=onehot_seq),
        grid=(m // tm, n // tn),
        in_specs=[
            pl.BlockSpec((tm, d), lambda i, j: (i, 0)),
            pl.BlockSpec((1, d), lambda i, j: (0, 0)),
            pl.BlockSpec((1, tn, d), lambda i, j: (layer, j, 0)),
        ],
        out_specs=pl.BlockSpec((tm, tn), lambda i, j: (i, j)),
        out_shape=jax.ShapeDtypeStruct((m, n), out_dtype),
        scratch_shapes=[pltpu.VMEM((tm, d), BF16)],
        compiler_params=_params("parallel", "arbitrary"),
    )(x, g.reshape(1, d), w_t)


def _norm_matmul_t_kernel(x_ref, g_ref, w_ref, o_ref):
    x = x_ref[...]
    ms = jnp.mean(x * x, axis=-1, keepdims=True)
    xn = (x * lax.rsqrt(ms + EPS) * g_ref[...]).astype(BF16)
    o_ref[...] = lax.dot_general(w_ref[0], xn, _NT, preferred_element_type=F32).astype(o_ref.dtype)


def _norm_matmul_t(x, g, w_t, layer, out_dtype, rows, tm=1024):
    d = x.shape[1]
    n = w_t.shape[1]
    return pl.pallas_call(
        _norm_matmul_t_kernel,
        grid=(rows // tm,),
        in_specs=[
            pl.BlockSpec((tm, d), lambda i: (i, 0)),
            pl.BlockSpec((1, d), lambda i: (0, 0)),
            pl.BlockSpec((1, n, d), lambda i: (layer, 0, 0)),
        ],
        out_specs=pl.BlockSpec((n, tm), lambda i: (0, i)),
        out_shape=jax.ShapeDtypeStruct((n, rows), out_dtype),
        compiler_params=_params("parallel"),
    )(x, g.reshape(1, d), w_t)


def _compress_kernel(x_ref, pe_ref, w1_ref, w2_ref, o_ref, *, tb):
    acc = jnp.zeros((tb, 2 * LANES), F32)
    for r in range(BLK):
        xr = x_ref[pl.ds(r, tb, stride=BLK), :] + pe_ref[0, r:r + 1, :]
        acc = acc + jnp.dot(xr.astype(BF16), w1_ref[0, r], preferred_element_type=F32)
    hid = jax.nn.gelu(acc)
    o_ref[...] = jnp.dot(hid.astype(BF16), w2_ref[0], preferred_element_type=F32)


def _compress(rows, n_blocks, pe2, w1bd, w2bd, col0=0, tb=128):
    assert col0 % LANES == 0
    return pl.pallas_call(
        functools.partial(_compress_kernel, tb=tb),
        grid=(n_blocks // tb, 4),
        in_specs=[
            pl.BlockSpec((tb * BLK, LANES), lambda i, c: (i, col0 // LANES + c)),
            pl.BlockSpec((1, BLK, LANES), lambda i, c: (c // 2, 0, 0)),
            pl.BlockSpec((1, BLK, LANES, 2 * LANES), lambda i, c: (c // 2, 0, 0, 0)),
            pl.BlockSpec((1, 2 * LANES, LANES), lambda i, c: (c // 2, 0, 0)),
        ],
        out_specs=pl.BlockSpec((tb, LANES), lambda i, c: (i, c)),
        out_shape=jax.ShapeDtypeStruct((n_blocks, 4 * LANES), F32),
        compiler_params=_params("parallel", "arbitrary"),
    )(rows, pe2, w1bd, w2bd)


def _compress_weights(pe_k, w1_k, w2_k, pe_v, w1_v, w2_v):
    def one(pe, w1, w2):
        hid = w1.shape[1]
        w1r = w1.reshape(BLK, HEAD_DIM, hid)
        z = jnp.zeros_like(w1r)
        w1bd = jnp.concatenate(
            [jnp.concatenate([w1r, z], axis=2), jnp.concatenate([z, w1r], axis=2)], axis=1)
        z2 = jnp.zeros_like(w2)
        w2bd = jnp.concatenate(
            [jnp.concatenate([w2, z2], axis=1), jnp.concatenate([z2, w2], axis=1)], axis=0)
        return jnp.concatenate([pe, pe], axis=1), w1bd.astype(BF16), w2bd.astype(BF16)

    k, v = one(pe_k, w1_k, w2_k), one(pe_v, w1_v, w2_v)
    return tuple(jnp.stack([a, b]) for a, b in zip(k, v))


def _compress_pages_kernel(pt_ref, *refs, n_pages):
    del pt_ref
    pages = refs[:n_pages]
    pe_ref, w1_ref, w2_ref, o_ref = refs[n_pages:]
    per_page = 2 * N_KV
    rows = per_page * n_pages
    ridx = lax.broadcasted_iota(jnp.int32, (rows, 1), 0)
    is_v = (_shr(ridx, N_KV) & 1) == 1
    acc = jnp.zeros((rows, 2 * LANES), F32)
    for d in range(HEAD_DIM):
        a = jnp.concatenate([pg[0, pl.ds(d, per_page, stride=HEAD_DIM), :] for pg in pages], axis=0)
        ak = jnp.where(is_v, 0.0, a + pe_ref[0, d:d + 1, :])
        av = jnp.where(is_v, a + pe_ref[1, d:d + 1, :], 0.0)
        lhs = jnp.concatenate([ak, av], axis=1).astype(BF16)
        acc = acc + jnp.dot(lhs, w1_ref[d], preferred_element_type=F32)
    hid = jax.nn.gelu(acc)
    lhs = jnp.concatenate([jnp.where(is_v, 0.0, hid), jnp.where(is_v, hid, 0.0)], axis=1)
    o_ref[...] = jnp.dot(lhs.astype(BF16), w2_ref[...], preferred_element_type=F32)


def _compress_pages(cache_t, page_table, layer, n_phys, pe_t, w1p, w2p):
    nseq, n_pages = page_table.shape
    rows = 2 * N_KV * n_pages

    def page_map(p_i):
        return lambda b, pt: (layer * n_phys + pt[b * n_pages + p_i], 0, 0)

    in_specs = [pl.BlockSpec((1,) + cache_t.shape[1:], page_map(p_i)) for p_i in range(n_pages)]
    in_specs += [
        pl.BlockSpec(pe_t.shape, lambda b, pt: (0, 0, 0)),
        pl.BlockSpec(w1p.shape, lambda b, pt: (0, 0, 0)),
        pl.BlockSpec(w2p.shape, lambda b, pt: (0, 0)),
    ]
    grid_spec = pltpu.PrefetchScalarGridSpec(
        num_scalar_prefetch=1,
        grid=(nseq,),
        in_specs=in_specs,
        out_specs=pl.BlockSpec((rows, LANES), lambda b, pt: (b, 0)),
    )
    return pl.pallas_call(
        functools.partial(_compress_pages_kernel, n_pages=n_pages),
        grid_spec=grid_spec,
        out_shape=jax.ShapeDtypeStruct((nseq * rows, LANES), F32),
        compiler_params=_params("arbitrary"),
    )(page_table.reshape(-1), *([cache_t] * n_pages), pe_t, w1p, w2p)


def _compress_page_weights(pe_k, w1_k, w2_k, pe_v, w1_v, w2_v):
    eye = jnp.eye(2, dtype=F32)
    hid = w1_k.shape[1]

    def w1p(w1):
        w1d = w1.reshape(BLK, HEAD_DIM, hid).transpose(1, 0, 2)
        return jnp.einsum('drj,bc->dbrcj', w1d, eye)

    def w2p(w2):
        return jnp.einsum('je,bc->bjce', w2, eye)

    w1 = jnp.stack([w1p(w1_k), w1p(w1_v)], axis=1).reshape(HEAD_DIM, 4 * BLK, 2 * hid)
    w2 = jnp.stack([w2p(w2_k), w2p(w2_v)]).reshape(4 * hid, 2 * HEAD_DIM)
    pe_t = jnp.stack([jnp.tile(pe_k.T, (1, 2)), jnp.tile(pe_v.T, (1, 2))])
    return pe_t, w1.astype(BF16), w2.astype(BF16)


def _pool_kernel(u_ref, halo_ref, w_ref, s_ref, o_ref, *, tt, tiles_per_seq):
    i = pl.program_id(0)
    u = u_ref[...]
    halo = halo_ref[...]
    if tiles_per_seq is not None:
        halo = jnp.where(i % tiles_per_seq == 0, 0.0, halo)
        pos = (i % tiles_per_seq) * tt + lax.broadcasted_iota(jnp.int32, (tt, 1), 0)
    a = jnp.concatenate([halo, u], axis=0)
    gw = u.shape[1] // len(POOL_WINDOWS)
    for gi, w in enumerate(POOL_WINDOWS):
        b = a[:, gi * gw:(gi + 1) * gw]
        width = 1
        while width < w:
            b = b[:b.shape[0] - width] + b[width:]
            width *= 2
        lo = POOL_HALO + 1 - w
        win = b[lo:lo + tt]
        if tiles_per_seq is None:
            mean = win / float(w)
        else:
            mean = win / jnp.minimum(pos + 1, w).astype(F32)
        d = mean - u[:, gi * gw:(gi + 1) * gw]
        y = jnp.dot(d.astype(BF16), w_ref[gi], preferred_element_type=F32)
        o_ref[:, gi * gw:(gi + 1) * gw] = (y * s_ref[:, gi * gw:(gi + 1) * gw]).astype(o_ref.dtype)


def _pool_mix(u, m, w_grp, scale, tt, tiles_per_seq, col=0):
    width = scale.shape[0]
    hb = tt // POOL_HALO
    return pl.pallas_call(
        functools.partial(_pool_kernel, tt=tt, tiles_per_seq=tiles_per_seq),
        grid=(m // tt,),
        in_specs=[
            pl.BlockSpec((tt, width), lambda i: (i, col)),
            pl.BlockSpec((POOL_HALO, width), lambda i: (jnp.maximum(i * hb - 1, 0), col)),
            pl.BlockSpec(w_grp.shape, lambda i: (0, 0, 0)),
            pl.BlockSpec((1, width), lambda i: (0, 0)),
        ],
        out_specs=pl.BlockSpec((tt, width), lambda i: (i, 0)),
        out_shape=jax.ShapeDtypeStruct((m, width), BF16),
        compiler_params=_params("parallel"),
    )(u, u, w_grp, scale.reshape(1, width))


def _gated_branch_kernel(a_ref, b_ref, wa_ref, wb_ref, x_ref, g_ref, wga_ref, wgb_ref, o_ref, xn_ref):
    @pl.when(pl.program_id(1) == 0)
    def _():
        x = x_ref[...]
        ms = jnp.mean(x * x, axis=-1, keepdims=True)
        xn_ref[...] = (x * lax.rsqrt(ms + EPS) * g_ref[...]).astype(BF16)

    xn = xn_ref[...]
    ga = jax.nn.sigmoid(lax.dot_general(xn, wga_ref[0], _NT, preferred_element_type=F32))
    gb = jax.nn.sigmoid(lax.dot_general(xn, wgb_ref[0], _NT, preferred_element_type=F32))
    a = jnp.dot(a_ref[...], wa_ref[0], preferred_element_type=F32)
    b = jnp.dot(b_ref[...], wb_ref[0], preferred_element_type=F32)
    o_ref[...] = (ga * a + gb * b).astype(o_ref.dtype)


def _gated_branches(a, b, wa, wb, x, g, wg_t, layer, tm=1024, tn=512):
    m, k = a.shape
    d = x.shape[1]
    n = wa.shape[2]
    nj = n // tn
    return pl.pallas_call(
        _gated_branch_kernel,
        grid=(m // tm, nj),
        in_specs=[
            pl.BlockSpec((tm, k), lambda i, j: (i, 0)),
            pl.BlockSpec((tm, k), lambda i, j: (i, 0)),
            pl.BlockSpec((1, k, tn), lambda i, j: (layer, 0, j)),
            pl.BlockSpec((1, k, tn), lambda i, j: (layer, 0, j)),
            pl.BlockSpec((tm, d), lambda i, j: (i, 0)),
            pl.BlockSpec((1, d), lambda i, j: (0, 0)),
            pl.BlockSpec((1, tn, d), lambda i, j: (layer, j, 0)),
            pl.BlockSpec((1, tn, d), lambda i, j: (layer, j + nj, 0)),
        ],
        out_specs=pl.BlockSpec((tm, tn), lambda i, j: (i, j)),
        out_shape=jax.ShapeDtypeStruct((m, n), BF16),
        scratch_shapes=[pltpu.VMEM((tm, d), BF16)],
        compiler_params=_params("parallel", "arbitrary"),
    )(a, b, wa, wb, x, g.reshape(1, d), wg_t, wg_t)


def _resid_matmul_kernel(x_ref, a_ref, w_ref, o_ref):
    o_ref[...] = x_ref[...] + jnp.dot(a_ref[...], w_ref[0], preferred_element_type=F32)


def _resid_matmul(x, a, w, layer, tm=1024, tn=512):
    m, k = a.shape
    n = w.shape[2]
    return pl.pallas_call(
        _resid_matmul_kernel,
        grid=(m // tm, n // tn),
        in_specs=[
            pl.BlockSpec((tm, tn), lambda i, j: (i, j)),
            pl.BlockSpec((tm, k), lambda i, j: (i, 0)),
            pl.BlockSpec((1, k, tn), lambda i, j: (layer, 0, j)),
        ],
        out_specs=pl.BlockSpec((tm, tn), lambda i, j: (i, j)),
        out_shape=jax.ShapeDtypeStruct((m, n), F32),
        compiler_params=_params("parallel", "arbitrary"),
    )(x, a, w)


def _mlp_kernel(x_ref, g_ref, wu_ref, wd_ref, gf_ref, o_ref, h_ref, *, final_norm):
    f = pl.program_id(1)

    @pl.when(f == 0)
    def _():
        x = x_ref[...]
        ms = jnp.mean(x * x, axis=-1, keepdims=True)
        h_ref[...] = (x * lax.rsqrt(ms + EPS) * g_ref[...]).astype(BF16)
        o_ref[...] = x

    up = jnp.dot(h_ref[...], wu_ref[0], preferred_element_type=F32)
    act = jnp.square(jnp.maximum(up, 0.0)).astype(BF16)
    o_ref[...] += jnp.dot(act, wd_ref[0], preferred_element_type=F32)

    if final_norm:
        @pl.when(f == pl.num_programs(1) - 1)
        def _():
            y = o_ref[...]
            ms = jnp.mean(y * y, axis=-1, keepdims=True)
            o_ref[...] = y * lax.rsqrt(ms + EPS) * gf_ref[...]


def _mlp(x, g, w_up, w_down, g_final, final_norm, layer, row0=0, rows=None, tm=1024, tf=512):
    d = x.shape[1]
    m = x.shape[0] if rows is None else rows
    dff = w_up.shape[2]
    assert row0 % tm == 0 and m % tm == 0
    return pl.pallas_call(
        functools.partial(_mlp_kernel, final_norm=final_norm),
        grid=(m // tm, dff // tf),
        in_specs=[
            pl.BlockSpec((tm, d), lambda i, f: (row0 // tm + i, 0)),
            pl.BlockSpec((1, d), lambda i, f: (0, 0)),
            pl.BlockSpec((1, d, tf), lambda i, f: (layer, 0, f)),
            pl.BlockSpec((1, tf, d), lambda i, f: (layer, f, 0)),
            pl.BlockSpec((1, d), lambda i, f: (0, 0)),
        ],
        out_specs=pl.BlockSpec((tm, d), lambda i, f: (i, 0)),
        out_shape=jax.ShapeDtypeStruct((m, d), F32),
        scratch_shapes=[pltpu.VMEM((tm, d), BF16)],
        compiler_params=_params("parallel", "arbitrary"),
    )(x, g.reshape(1, d), w_up, w_down, g_final.reshape(1, d))


def _shr(x, pow2):
    assert pow2 & (pow2 - 1) == 0
    return lax.shift_right_logical(x, pow2.bit_length() - 1)


def _masked_softmax(s, mask):
    sm = jnp.where(mask, s, NEG)
    e = jnp.exp(sm - jnp.max(sm, axis=-1, keepdims=True))
    return e / jnp.sum(e, axis=-1, keepdims=True)


def _select_blocks(imp, qpos, n_loop):
    r, nb = imp.shape
    bidx = lax.broadcasted_iota(jnp.int32, (r, nb), 1)
    cur = _shr(qpos, BLK)
    forced = (bidx == 0) | (bidx == cur) | (bidx == cur - 1)
    cand = bidx * BLK <= qpos
    score = jnp.where(cand, jnp.where(forced, FORCED, imp), NEG)
    rank = jnp.zeros((r, nb), F32)
    for j in range(n_loop):
        c = score[:, j:j + 1]
        beats = (c > score) | ((c == score) & (bidx > j))
        rank = rank + jnp.where(beats, 1.0, 0.0)
    return jnp.where(cand & (rank < TOP_N), 1.0, 0.0)


def _select_blocks_t(imp_t, qpos_t):
    nb, r = imp_t.shape
    bidx = lax.broadcasted_iota(jnp.int32, (nb, r), 0)
    cur = _shr(qpos_t, BLK)
    forced = (bidx == 0) | (bidx == cur) | (bidx == cur - 1)
    cand = bidx * BLK <= qpos_t
    score = jnp.where(cand, jnp.where(forced, FORCED, imp_t), NEG)
    n_chunks = nb // SUBLANES
    chunks = [score[SUBLANES * c:SUBLANES * (c + 1)] for c in range(n_chunks)]
    ranks = [jnp.zeros((SUBLANES, r), F32) for _ in range(n_chunks)]
    sub = lax.broadcasted_iota(jnp.int32, (SUBLANES, r), 0)
    for j in range(nb):
        cj = score[j:j + 1, :]
        for c in range(n_chunks):
            lo = SUBLANES * c
            ge = jnp.where(cj >= chunks[c], 1.0, 0.0)
            gt = jnp.where(cj > chunks[c], 1.0, 0.0)
            if lo > j:
                term = ge
            elif lo + SUBLANES - 1 < j:
                term = gt
            else:
                term = jnp.where(sub > j - lo, ge, gt)
            ranks[c] = ranks[c] + term
    rank = jnp.concatenate(ranks, axis=0)
    return jnp.where(cand & (rank < TOP_N), 1.0, 0.0)


def _block_expand(n_blocks, k0, n_keys):
    b = lax.broadcasted_iota(jnp.int32, (n_blocks, n_keys), 0)
    k = lax.broadcasted_iota(jnp.int32, (n_blocks, n_keys), 1) + k0
    return jnp.where(_shr(k, BLK) == b, 1.0, 0.0).astype(BF16)


def _softmax_keys(scores, mask):
    sm = jnp.where(mask, scores, NEG)
    e = jnp.exp(sm - jnp.max(sm, axis=0, keepdims=True))
    return e / jnp.sum(e, axis=0, keepdims=True)


def _nsa_prompt_kernel(q_ref, kc_ref, vc_ref, ks_ref, vs_ref, kw_ref, vw_ref, g_ref,
                       o_ref, *, seq, heads):
    i = pl.program_id(2)
    rows = GROUP * Q_TILE
    hw = GROUP * HEAD_DIM
    nb = seq // BLK
    qpos = i * Q_TILE + (lax.broadcasted_iota(jnp.int32, (1, rows), 1) & (Q_TILE - 1))
    bcol = lax.broadcasted_iota(jnp.int32, (nb, 1), 0)
    bidx = lax.broadcasted_iota(jnp.int32, (1, nb), 1)
    cmask = bcol * BLK + (BLK - 1) <= qpos
    own = _shr(bidx, Q_TILE // BLK) == i

    def prepare(h):
        qt = q_ref[:, h * hw:(h + 1) * hw].astype(BF16)
        q = jnp.concatenate([qt[:, g * HEAD_DIM:(g + 1) * HEAD_DIM] for g in range(GROUP)], axis=0)
        s_c = lax.dot_general(kc_ref[0, h], q, _NT, preferred_element_type=F32)
        p_c = jnp.where(cmask, _softmax_keys(s_c, cmask), 0.0)
        o_cmp = jnp.dot(vc_ref[0, h], p_c.astype(BF16), preferred_element_type=F32)
        imp = p_c[:, 0:Q_TILE]
        for g in range(1, GROUP):
            imp = imp + p_c[:, g * Q_TILE:(g + 1) * Q_TILE]
        sel = _select_blocks_t(imp, qpos[:, 0:Q_TILE]).T
        block_bias = jnp.where((sel > 0.5) & jnp.logical_not(own), 0.0, NEG).astype(BF16)
        if nb < HEAD_DIM:
            block_bias = jnp.concatenate([block_bias, jnp.zeros((Q_TILE, HEAD_DIM - nb), BF16)], axis=1)
        q_aug = jnp.concatenate([q, jnp.concatenate([block_bias] * GROUP, axis=0)], axis=1)
        q_pad = jnp.concatenate([q, jnp.zeros((rows, HEAD_DIM), BF16)], axis=1)
        return q_pad, q_aug, o_cmp

    def online_step(carry, scores, v_t):
        m, l, acc = carry
        m_new = jnp.maximum(m, jnp.max(scores, axis=0, keepdims=True))
        alpha = jnp.exp(m - m_new)
        pj = jnp.exp(scores - m_new)
        l = alpha * l + jnp.sum(pj, axis=0, keepdims=True)
        acc = alpha * acc + jnp.dot(v_t, pj.astype(BF16), preferred_element_type=F32)
        return m_new, l, acc

    prepared = [prepare(h) for h in range(heads)]

    def body(j, carries):
        k0 = pl.multiple_of(j * KEY_TILE, KEY_TILE)
        scores = [lax.dot_general(ks_ref[pl.ds(k0, KEY_TILE), h * LANES:(h + 1) * LANES],
                                  prepared[h][1], _NT, preferred_element_type=F32)
                  for h in range(heads)]
        return tuple(online_step(carries[h], scores[h],
                                 vs_ref[h * HEAD_DIM:(h + 1) * HEAD_DIM, pl.ds(k0, KEY_TILE)])
                     for h in range(heads))

    n_tiles = lax.div(i * Q_TILE + KEY_TILE - 1, KEY_TILE)
    init = tuple((jnp.full((1, rows), NEG, F32), jnp.zeros((1, rows), F32),
                  jnp.zeros((HEAD_DIM, rows), F32)) for _ in range(heads))
    carries = lax.fori_loop(0, n_tiles, body, init)

    d0 = pl.multiple_of(i * Q_TILE, Q_TILE)
    causal = d0 + lax.broadcasted_iota(jnp.int32, (Q_TILE, 1), 0) <= qpos
    span = WINDOW + Q_TILE
    start = pl.multiple_of(jnp.maximum(i * Q_TILE - WINDOW, 0), Q_TILE)
    wpos = start + lax.broadcasted_iota(jnp.int32, (span, 1), 0)
    wmask = (wpos <= qpos) & (wpos >= qpos - WINDOW)
    sd = [lax.dot_general(ks_ref[pl.ds(d0, Q_TILE), h * LANES:(h + 1) * LANES], prepared[h][0],
                          _NT, preferred_element_type=F32) for h in range(heads)]
    sw = [lax.dot_general(kw_ref[pl.ds(start, span), h * LANES:(h + 1) * LANES], prepared[h][0],
                          _NT, preferred_element_type=F32) for h in range(heads)]
    for h in range(heads):
        o_cmp = prepared[h][2]
        _, l, acc = online_step(carries[h], jnp.where(causal, sd[h], NEG),
                                vs_ref[h * HEAD_DIM:(h + 1) * HEAD_DIM, pl.ds(d0, Q_TILE)])
        o_sel = acc / l
        vw_t = vw_ref[h * HEAD_DIM:(h + 1) * HEAD_DIM, pl.ds(start, span)]
        o_win = jnp.dot(vw_t, _softmax_keys(sw[h], wmask).astype(BF16), preferred_element_type=F32)

        g_t = g_ref[:, h * LANES:(h + 1) * LANES].T
        for g in range(GROUP):
            sl = slice(g * Q_TILE, (g + 1) * Q_TILE)
            o_g = (g_t[g:g + 1] * o_cmp[:, sl] + g_t[GROUP + g:GROUP + g + 1] * o_sel[:, sl]
                   + g_t[2 * GROUP + g:2 * GROUP + g + 1] * o_win[:, sl])
            c0 = h * hw + g * HEAD_DIM
            o_ref[:, c0:c0 + HEAD_DIM] = o_g.T.astype(o_ref.dtype)


def _nsa_prompt(q, q_col, kc, vc, keys, vals_t, gates, gate_col, n, seq, heads=4):
    nq = seq // Q_TILE
    nb = seq // BLK
    assert nb <= HEAD_DIM
    row = lambda b, h, i: (b * nq + i, h)
    head = lambda b, h, i: (b, h, 0, 0)
    steps = N_KV // heads
    return pl.pallas_call(
        functools.partial(_nsa_prompt_kernel, seq=seq, heads=heads),
        grid=(n, steps, nq),
        in_specs=[
            pl.BlockSpec((Q_TILE, heads * GROUP * HEAD_DIM), lambda b, h, i: (b * nq + i, q_col * steps + h)),
            pl.BlockSpec((1, heads, nb, HEAD_DIM), head), pl.BlockSpec((1, heads, HEAD_DIM, nb), head),
            pl.BlockSpec((seq, heads * LANES), lambda b, h, i: (b, h)),
            pl.BlockSpec((heads * HEAD_DIM, seq), lambda b, h, i: (h, b)),
            pl.BlockSpec((seq, heads * LANES), lambda b, h, i: (b, steps + h)),
            pl.BlockSpec((heads * HEAD_DIM, seq), lambda b, h, i: (steps + h, b)),
            pl.BlockSpec((Q_TILE, heads * LANES), lambda b, h, i: (b * nq + i, gate_col * steps + h)),
        ],
        out_specs=pl.BlockSpec((Q_TILE, heads * GROUP * HEAD_DIM), row),
        out_shape=jax.ShapeDtypeStruct((n * seq, N_KV * GROUP * HEAD_DIM), BF16),
        compiler_params=_params("parallel", "parallel", "arbitrary"),
    )(q, kc, vc, keys, vals_t, keys, vals_t, gates)


def _nsa_sample_kernel(pt_ref, q_ref, g_ref, *refs, n_pages, page, dec, wb, nsq):
    del pt_ref
    cmp_seq, cmp_new = refs[:2]
    sel_pages = refs[2:nsq * n_pages + 2]
    sel_new = refs[nsq * n_pages + 2]
    win_ref, win_new, _, o_ref, wo_ref, kc_ref, k_ref, v_ref, kw_ref, vw_ref = refs[nsq * n_pages + 3:]
    seqs = range(nsq)
    past = n_pages * page
    rows = N_KV * GROUP * dec
    ridx = lax.broadcasted_iota(jnp.int32, (rows, 1), 0)
    qpos = past + (ridx & (dec - 1))
    q = [q_ref[s] for s in seqs]

    n_keys = past + page
    for s in seqs:
        for p_i in range(n_pages + 1):
            pg = sel_pages[s * n_pages + p_i][0] if p_i < n_pages else sel_new[s]
            k_ref[s, :, p_i * page:(p_i + 1) * page] = pg[:KV_W].astype(BF16)
            v_ref[s, :, p_i * page:(p_i + 1) * page] = pg[KV_W:].astype(BF16)
    wn = [win_ref[s] for s in seqs]
    nw = [win_new[s] for s in seqs]
    for s in seqs:
        kw_ref[s, :, 0:wb] = wn[s][:KV_W].astype(BF16)
        vw_ref[s, :, 0:wb] = wn[s][KV_W:].astype(BF16)
        kw_ref[s, :, wb:wb + page] = nw[s][:KV_W].astype(BF16)
        vw_ref[s, :, wb:wb + page] = nw[s][KV_W:].astype(BF16)

    bpp = page // BLK
    nbp = kc_ref.shape[1]
    for s in seqs:
        kc_ref[s] = jnp.zeros(kc_ref.shape[1:], F32)
        kc_ref[s, 0:bpp * n_pages, :] = cmp_seq[s]
        kc_ref[s, bpp * n_pages:bpp * n_pages + 1, :] = cmp_new[s, 0:1, :]
    sc = [lax.dot_general(q[s], kc_ref[s, :, :KV_W].astype(BF16), _NT, preferred_element_type=F32)
          for s in seqs]
    bidx = lax.broadcasted_iota(jnp.int32, (1, nbp), 1)
    cmask = bidx * BLK + (BLK - 1) <= qpos
    p = [jnp.where(cmask, _masked_softmax(sc[s], cmask), 0.0) for s in seqs]
    o_cmp = [jnp.dot(p[s].astype(BF16), kc_ref[s, :, KV_W:].astype(BF16), preferred_element_type=F32)
             for s in seqs]

    qpos_ht = past + (lax.broadcasted_iota(jnp.int32, (N_KV * dec, 1), 0) & (dec - 1))
    sel_rows = []
    for s in seqs:
        imps = []
        for h in range(N_KV):
            base = h * GROUP * dec
            acc = p[s][base:base + dec]
            for g in range(1, GROUP):
                acc = acc + p[s][base + g * dec:base + (g + 1) * dec]
            imps.append(acc)
        imp = jnp.concatenate(imps, axis=0)
        sel = _select_blocks(imp, qpos_ht, bpp * n_pages + 1)
        sel_rows.append(jnp.concatenate(
            [sel[h * dec:(h + 1) * dec] for h in range(N_KV) for _ in range(GROUP)], axis=0))

    sw = [jnp.dot(q[s], kw_ref[s], preferred_element_type=F32) for s in seqs]
    wpos = past - wb + lax.broadcasted_iota(jnp.int32, (1, wb + page), 1)
    wmask = (wpos <= qpos) & (wpos >= qpos - WINDOW)
    pw = [_masked_softmax(sw[s], wmask) for s in seqs]
    o_win = [lax.dot_general(pw[s].astype(BF16), vw_ref[s], _NT, preferred_element_type=F32)
             for s in seqs]

    ss = [jnp.dot(q[s], k_ref[s], preferred_element_type=F32) for s in seqs]
    expand = _block_expand(nbp, 0, n_keys)
    key_sel = [jnp.dot(sel_rows[s].astype(BF16), expand, preferred_element_type=F32) for s in seqs]
    kpos = lax.broadcasted_iota(jnp.int32, (1, n_keys), 1)
    ps = [_masked_softmax(ss[s], (key_sel[s] > 0.5) & (kpos <= qpos)) for s in seqs]
    o_sel = [lax.dot_general(ps[s].astype(BF16), v_ref[s], _NT, preferred_element_type=F32)
             for s in seqs]

    col = lax.broadcasted_iota(jnp.int32, (1, KV_W), 1)
    own_head = _shr(ridx, GROUP * dec) == _shr(col, HEAD_DIM)
    lane = lax.broadcasted_iota(jnp.int32, (1, page), 1)
    hr = GROUP * dec
    for s in seqs:
        gt = g_ref[s]
        o = gt[:, 0:1] * o_cmp[s] + gt[:, 1:2] * o_sel[s] + gt[:, 2:3] * o_win[s]
        o = jnp.where(own_head, o, 0.0)
        out = o[0:hr]
        for h in range(1, N_KV):
            out = out + o[h * hr:(h + 1) * hr]
        o_ref[s] = out

        shifted = pltpu.roll(wn[s], wb - dec, 1)
        tail = pltpu.roll(nw[s], page - dec, 1)
        wo_ref[s, :, 0:wb - page] = shifted[:, 0:wb - page]
        wo_ref[s, :, wb - page:wb] = jnp.where(lane >= page - dec, tail, shifted[:, wb - page:wb])


def _nsa_sample(page_table, layer, n_phys, qbd, gates, cmp_seq, cmp_new, sel_cache_t, sel_new,
                win_cache_t, win_new, win_out, nsq=2):
    nseq, n_pages = page_table.shape
    page = sel_cache_t.shape[2]
    rows = qbd.shape[1]
    dec = rows // (N_KV * GROUP)
    wb = win_cache_t.shape[2]
    bpp = page // BLK
    nbp = LANES
    assert bpp * n_pages + 1 <= nbp and wb == WINDOW and page == LANES and dec <= SUBLANES
    assert nseq % nsq == 0
    steps = nseq // nsq

    def page_map(s, p_i):
        return lambda b, pt: (layer * n_phys + pt[(b * nsq + s) * n_pages + p_i], 0, 0)

    seq_map = lambda b, pt: (b, 0, 0)
    layer_seq_map = lambda b, pt: (layer * steps + b, 0, 0)
    in_specs = [
        pl.BlockSpec((nsq, rows, KV_W), seq_map),
        pl.BlockSpec((nsq, rows, 8), seq_map),
        pl.BlockSpec((nsq, bpp * n_pages, 2 * KV_W), seq_map),
        pl.BlockSpec((nsq, 8, 2 * KV_W), seq_map),
    ]
    in_specs += [pl.BlockSpec((1, 2 * KV_W, page), page_map(s, p_i))
                 for s in range(nsq) for p_i in range(n_pages)]
    in_specs += [
        pl.BlockSpec((nsq, 2 * KV_W, page), seq_map),
        pl.BlockSpec((nsq, 2 * KV_W, wb), layer_seq_map),
        pl.BlockSpec((nsq, 2 * KV_W, page), seq_map),
        pl.BlockSpec(memory_space=pl.ANY),
    ]
    n_keys = (n_pages + 1) * page
    grid_spec = pltpu.PrefetchScalarGridSpec(
        num_scalar_prefetch=1,
        grid=(steps,),
        in_specs=in_specs,
        out_specs=[
            pl.BlockSpec((nsq, GROUP * dec, KV_W), seq_map),
            pl.BlockSpec((nsq, 2 * KV_W, wb), layer_seq_map),
        ],
        scratch_shapes=[
            pltpu.VMEM((nsq, nbp, 2 * KV_W), F32),
            pltpu.VMEM((nsq, KV_W, n_keys), BF16),
            pltpu.VMEM((nsq, KV_W, n_keys), BF16),
            pltpu.VMEM((nsq, KV_W, wb + page), BF16),
            pltpu.VMEM((nsq, KV_W, wb + page), BF16),
        ],
    )
    args = (page_table.reshape(-1), qbd, gates, cmp_seq, cmp_new,
            *([sel_cache_t] * (nsq * n_pages)), sel_new, win_cache_t, win_new, win_out)
    return pl.pallas_call(
        functools.partial(_nsa_sample_kernel, n_pages=n_pages, page=page, dec=dec, wb=wb, nsq=nsq),
        grid_spec=grid_spec,
        out_shape=[
            jax.ShapeDtypeStruct((nseq, GROUP * dec, KV_W), F32),
            jax.ShapeDtypeStruct(win_out.shape, F32),
        ],
        input_output_aliases={len(args) - 1: 1},
        compiler_params=_params("arbitrary"),
    )(*args)


def _heads_major(a, n, seq):
    return a.reshape(n, seq, N_KV, HEAD_DIM).transpose(0, 2, 1, 3).astype(BF16)


def _heads_major_t(a, n, seq):
    return a.reshape(n, seq, N_KV, HEAD_DIM).transpose(0, 2, 3, 1).astype(BF16)


def _pad_rows(a, rows):
    return jnp.pad(a, ((0, 0), (0, rows - a.shape[1]), (0, 0)))


def _channel_major(cache):
    depth, a, rows = cache.shape[:3]
    return cache.transpose(0, 1, 3, 4, 5, 2).reshape(depth * a, 2 * KV_W, rows)


def kernel(x_prompt, x_sample, cache_cmp_kv, cache_sel_kv, cache_win_kv, state_pool, page_table, g_mix, w_in, pe_ck, w_ck1, w_ck2, pe_cv, w_cv1, w_cv2, w_pool_grp, pool_scale, w_br_nsa, w_br_pool, w_o, g_mlp, w_up, w_down, g_final):
    n, seq, d = x_prompt.shape
    nseq, dec, _ = x_sample.shape
    depth = w_in.shape[0]
    n_phys, page = cache_cmp_kv.shape[1], cache_cmp_kv.shape[2]
    wb = cache_win_kv.shape[2]
    mp, ms = n * seq, nseq * dec
    nsa_w = N_KV * GROUP * HEAD_DIM
    kv_cols = 2 * KV_W
    n_gate = 3 * N_KV * GROUP
    pool_w = state_pool.shape[-1]
    pool_state = state_pool.shape[2]
    c_q, c_kv, c_ng = nsa_w, nsa_w + 3 * kv_cols, nsa_w + 3 * kv_cols + n_gate
    c_u = c_ng + pool_w

    x = jnp.concatenate([x_prompt.reshape(mp, d), x_sample.reshape(ms, d)], axis=0)
    kv_shape = (2, N_KV, HEAD_DIM)
    outs = [[] for _ in range(8)]
    cmp_cache_t = _channel_major(cache_cmp_kv)
    sel_cache_t = _channel_major(cache_sel_kv)
    win_cache_t = _channel_major(cache_win_kv)
    win_out = jnp.zeros(win_cache_t.shape, F32)

    w_t = w_in.transpose(0, 2, 1)
    w_q = (w_t[:, :c_q] * HEAD_DIM ** -0.5).astype(BF16)
    w_kv = w_t[:, c_q:c_kv].astype(BF16)
    w_ng = w_t[:, c_kv:c_ng].reshape(depth, 3, N_KV, GROUP, d).transpose(0, 2, 1, 3, 4)
    w_ng = jnp.pad(w_ng.reshape(depth, N_KV, 3 * GROUP, d), ((0, 0), (0, 0), (0, LANES - 3 * GROUP), (0, 0)))
    w_ng = w_ng.reshape(depth, N_KV * LANES, d).astype(BF16)
    w_u = w_t[:, c_ng:c_u].astype(BF16)
    w_mg = w_t[:, c_u:].astype(BF16)
    w_cat = jnp.concatenate([w_q, w_u, w_kv, w_ng], axis=1)
    o_u, o_kv, o_ng = nsa_w, nsa_w + pool_w, nsa_w + pool_w + 3 * kv_cols
    assert nsa_w == N_KV * GROUP * HEAD_DIM and o_u % pool_w == 0 and o_ng % (N_KV * LANES) == 0
    def key_slots(rows_t):
        r = rows_t.reshape(depth, N_KV, HEAD_DIM, d)
        return jnp.pad(r, ((0, 0), (0, 0), (0, LANES - HEAD_DIM), (0, 0))).reshape(depth, N_KV * LANES, d)

    c_sel, c_win = c_q + kv_cols, c_q + 2 * kv_cols
    w_keys = jnp.concatenate([key_slots(w_t[:, c_sel:c_sel + KV_W]),
                              key_slots(w_t[:, c_win:c_win + KV_W])], axis=1).astype(BF16)
    w_vals = jnp.concatenate([w_t[:, c_sel + KV_W:c_sel + kv_cols],
                              w_t[:, c_win + KV_W:c_win + kv_cols]], axis=1).astype(BF16)
    w_bn, w_bp, w_ob = w_br_nsa.astype(BF16), w_br_pool.astype(BF16), w_o.astype(BF16)
    w_upb, w_downb = w_up.astype(BF16), w_down.astype(BF16)

    for l in range(depth):
        z = _norm_matmul(x, g_mix[l], w_cat, l, F32, sigmoid_tiles=1)
        zkv = z[:, o_kv:o_ng]
        q = z[:, :nsa_w]
        ng = z[:, o_ng:]

        cmp_w = (pe_ck[l], w_ck1[l], w_ck2[l], pe_cv[l], w_cv1[l], w_cv2[l])
        cw = _compress_weights(*cmp_w)

        cmp_p = _compress(z, mp // BLK, *cw, col0=o_kv)
        kc_p = _heads_major(cmp_p[:, :KV_W], n, seq // BLK)
        vc_p = _heads_major_t(cmp_p[:, KV_W:], n, seq // BLK)
        zp = zkv[:mp]
        keys_p = _norm_matmul(x, g_mix[l], w_keys, l, BF16, onehot_seq=seq, rows=mp)
        vals_p = _norm_matmul_t(x, g_mix[l], w_vals, l, BF16, mp)
        o_nsa_p = _nsa_prompt(z, 0, kc_p, vc_p, keys_p, vals_p, z, o_ng // (N_KV * LANES), n, seq)

        zs = zkv[mp:].reshape(nseq, dec, 3 * kv_cols)
        new_c = _pad_rows(zs[:, :, :kv_cols], page)
        lanes_t = lambda a: jnp.pad(a.transpose(0, 2, 1), ((0, 0), (0, 0), (0, page - dec)))
        new_s = lanes_t(zs[:, :, kv_cols:2 * kv_cols])
        new_w = lanes_t(zs[:, :, 2 * kv_cols:])
        bpp = page // BLK
        cmp_seq = _compress_pages(cmp_cache_t, page_table, l, n_phys, *_compress_page_weights(*cmp_w))
        n_pages = page_table.shape[1]
        cmp_seq = cmp_seq.reshape(nseq, n_pages, 2, N_KV, bpp, HEAD_DIM).transpose(0, 1, 4, 2, 3, 5)
        cmp_seq = cmp_seq.reshape(nseq, n_pages * bpp, kv_cols)
        cmp_new = _compress(new_c.reshape(nseq * page, kv_cols), nseq * bpp, *cw)
        cmp_new = _pad_rows(cmp_new.reshape(nseq, bpp, kv_cols)[:, :1], 8)

        qs = q[mp:].astype(BF16).reshape(nseq, dec, N_KV, GROUP, HEAD_DIM)
        eye = jnp.eye(N_KV, dtype=BF16)
        qbd = jnp.einsum('bthgd,hk->bhgtkd', qs, eye).reshape(nseq, N_KV * GROUP * dec, KV_W)
        gs = ng[mp:].reshape(nseq, dec, N_KV, LANES)[..., :3 * GROUP].reshape(nseq, dec, N_KV, 3, GROUP)
        gs = gs.transpose(0, 2, 4, 1, 3).reshape(nseq, N_KV * GROUP * dec, 3)
        gs = jnp.pad(gs, ((0, 0), (0, 0), (0, 5)))
        o_s, win_out = _nsa_sample(page_table, l, n_phys, qbd, gs, cmp_seq, cmp_new, sel_cache_t, new_s,
                                   win_cache_t, new_w, win_out)
        o_nsa_s = o_s.reshape(nseq, GROUP, dec, N_KV, HEAD_DIM).transpose(0, 2, 3, 1, 4)
        o_nsa = jnp.concatenate([o_nsa_p, o_nsa_s.reshape(ms, nsa_w).astype(BF16)], axis=0)

        w_grp = w_pool_grp[l].astype(BF16)
        u = z[:, o_u:o_kv]
        u_p, u_s = u[:mp], u[mp:].reshape(nseq, dec, pool_w)
        o_pool_p = _pool_mix(z, mp, w_grp, pool_scale[l], 512, seq // 512, col=o_u // pool_w)
        hist = jnp.pad(state_pool[l], ((0, 0), (POOL_HALO - pool_state, 0), (0, 0)))
        xx_s = jnp.concatenate([hist, u_s], axis=1)
        rows_s = POOL_HALO + dec
        o_pool_s = _pool_mix(xx_s.reshape(nseq * rows_s, pool_w), nseq * rows_s, w_grp, pool_scale[l],
                             16 * rows_s, None)
        o_pool_s = o_pool_s.reshape(nseq, rows_s, pool_w)[:, POOL_HALO:].reshape(ms, pool_w)
        o_pool = jnp.concatenate([o_pool_p, o_pool_s], axis=0)

        m = _gated_branches(o_nsa, o_pool, w_bn, w_bp, x, g_mix[l], w_mg, l)
        x1 = _resid_matmul(x, m, w_ob, l)
        if l < depth - 1:
            x = _mlp(x1, g_mlp[l], w_upb, w_downb, g_final, False, l)
        else:
            y_prompt = _mlp(x1, g_mlp[l], w_upb, w_downb, g_final, True, l, 0, mp)
            y_sample = _mlp(x1, g_mlp[l], w_upb, w_downb, g_final, True, l, mp, ms)

        zp6 = zp.reshape(n, seq, 3, *kv_shape)
        outs[0].append(zp6[:, :, 0])
        outs[1].append(zp6[:, :, 1])
        outs[2].append(zp6[:, -min(WINDOW, seq):, 2])
        outs[3].append(u_p.reshape(n, seq, pool_w)[:, -pool_state:])
        zs6 = zs.reshape(nseq, dec, 3, *kv_shape)
        outs[4].append(zs6[:, :, 0])
        outs[5].append(zs6[:, :, 1])
        outs[7].append(xx_s[:, -pool_state:])

    y_prompt = y_prompt.reshape(n, seq, d)
    y_sample = y_sample.reshape(nseq, dec, d)
    s_win = win_out.reshape(depth, nseq, *kv_shape, wb).transpose(0, 1, 5, 2, 3, 4)
    stacked = [s_win if i == 6 else jnp.stack(o) for i, o in enumerate(outs)]
    return (y_prompt, y_sample) + tuple(stacked)
```

```python
import functools

import jax
import jax.numpy as jnp
from jax import lax
from jax.experimental import pallas as pl
from jax.experimental.pallas import tpu as pltpu

F32 = jnp.float32
BF16 = jnp.bfloat16

N_KV = 4
GROUP = 4
HEAD_DIM = 64
KV_W = N_KV * HEAD_DIM
BLK = 64
TOP_N = 16
WINDOW = 512
Q_TILE = 128
KEY_TILE = 512
POOL_WINDOWS = (2, 4, 8, 16)
POOL_HALO = 16
EPS = 1e-6
NEG = -1e30
FORCED = 1e4
LANES = 128
SUBLANES = 8
VMEM_LIMIT = 56 * 1024 * 1024

_NT = (((1,), (1,)), ((), ()))


def _params(*sem):
    return pltpu.CompilerParams(dimension_semantics=sem, vmem_limit_bytes=VMEM_LIMIT)


def _norm_matmul_kernel(x_ref, g_ref, w_ref, o_ref, xn_ref, *, sigmoid_tiles, onehot_seq):
    i, j = pl.program_id(0), pl.program_id(1)
    tm, tn = o_ref.shape

    @pl.when(j == 0)
    def _():
        x = x_ref[...]
        ms = jnp.mean(x * x, axis=-1, keepdims=True)
        xn_ref[...] = (x * lax.rsqrt(ms + EPS) * g_ref[...]).astype(BF16)

    z = lax.dot_general(xn_ref[...], w_ref[0], _NT, preferred_element_type=F32)
    first_sigmoid = pl.num_programs(1) - sigmoid_tiles
    first_plain = 0

    if onehot_seq is not None:
        first_plain = 1

        @pl.when(j == 0)
        def _():
            blk = _shr((i * tm + lax.broadcasted_iota(jnp.int32, (tm, 1), 0)) & (onehot_seq - 1), BLK)
            lane = lax.broadcasted_iota(jnp.int32, (1, tn), 1) & (LANES - 1)
            o_ref[...] = jnp.where(lane - HEAD_DIM == blk, 1.0, z).astype(o_ref.dtype)

    @pl.when((j >= first_plain) & (j < first_sigmoid))
    def _():
        o_ref[...] = z.astype(o_ref.dtype)

    @pl.when(j >= first_sigmoid)
    def _():
        o_ref[...] = jax.nn.sigmoid(z).astype(o_ref.dtype)


def _norm_matmul(x, g, w_t, layer, out_dtype, sigmoid_tiles=0, onehot_seq=None, rows=None,
                 tm=1024, tn=512):
    m, d = x.shape
    m = m if rows is None else rows
    n = w_t.shape[1]
    tm, tn = min(tm, m), min(tn, n)
    return pl.pallas_call(
        functools.partial(_norm_matmul_kernel, sigmoid_tiles=sigmoid_tiles, onehot_seq=onehot_seq),
        grid=(m // tm, n // tn),
        in_specs=[
            pl.BlockSpec((tm, d), lambda i, j: (i, 0)),
            pl.BlockSpec((1, d), lambda i, j: (0, 0)),
            pl.BlockSpec((1, tn, d), lambda i, j: (layer, j, 0)),
        ],
        out_specs=pl.BlockSpec((tm, tn), lambda i, j: (i, j)),
        out_shape=jax.ShapeDtypeStruct((m, n), out_dtype),
        scratch_shapes=[pltpu.VMEM((tm, d), BF16)],
        compiler_params=_params("parallel", "arbitrary"),
    )(x, g.reshape(1, d), w_t)


def _norm_matmul_t_kernel(x_ref, g_ref, w_ref, o_ref):
    x = x_ref[...]
    ms = jnp.mean(x * x, axis=-1, keepdims=True)
    xn = (x * lax.rsqrt(ms + EPS) * g_ref[...]).astype(BF16)
    o_ref[...] = lax.dot_general(w_ref[0], xn, _NT, preferred_element_type=F32).astype(o_ref.dtype)


def _norm_matmul_t(x, g, w_t, layer, out_dtype, rows, tm=1024):
    d = x.shape[1]
    n = w_t.shape[1]
    return pl.pallas_call(
        _norm_matmul_t_kernel,
        grid=(rows // tm,),
        in_specs=[
            pl.BlockSpec((tm, d), lambda i: (i, 0)),
            pl.BlockSpec((1, d), lambda i: (0, 0)),
            pl.BlockSpec((1, n, d), lambda i: (layer, 0, 0)),
        ],
        out_specs=pl.BlockSpec((n, tm), lambda i: (0, i)),
        out_shape=jax.ShapeDtypeStruct((n, rows), out_dtype),
        compiler_params=_params("parallel"),
    )(x, g.reshape(1, d), w_t)


def _compress_kernel(x_ref, pe_ref, w1_ref, w2_ref, o_ref, *, tb):
    acc = jnp.zeros((tb, 2 * LANES), F32)
    for r in range(BLK):
        xr = x_ref[pl.ds(r, tb, stride=BLK), :] + pe_ref[0, r:r + 1, :]
        acc = acc + jnp.dot(xr.astype(BF16), w1_ref[0, r], preferred_element_type=F32)
    hid = jax.nn.gelu(acc)
    o_ref[...] = jnp.dot(hid.astype(BF16), w2_ref[0], preferred_element_type=F32)


def _compress(rows, n_blocks, pe2, w1bd, w2bd, col0=0, tb=128):
    assert col0 % LANES == 0
    return pl.pallas_call(
        functools.partial(_compress_kernel, tb=tb),
        grid=(n_blocks // tb, 4),
        in_specs=[
            pl.BlockSpec((tb * BLK, LANES), lambda i, c: (i, col0 // LANES + c)),
            pl.BlockSpec((1, BLK, LANES), lambda i, c: (c // 2, 0, 0)),
            pl.BlockSpec((1, BLK, LANES, 2 * LANES), lambda i, c: (c // 2, 0, 0, 0)),
            pl.BlockSpec((1, 2 * LANES, LANES), lambda i, c: (c // 2, 0, 0)),
        ],
        out_specs=pl.BlockSpec((tb, LANES), lambda i, c: (i, c)),
        out_shape=jax.ShapeDtypeStruct((n_blocks, 4 * LANES), F32),
        compiler_params=_params("parallel", "arbitrary"),
    )(rows, pe2, w1bd, w2bd)


def _compress_weights(pe_k, w1_k, w2_k, pe_v, w1_v, w2_v):
    def one(pe, w1, w2):
        hid = w1.shape[1]
        w1r = w1.reshape(BLK, HEAD_DIM, hid)
        z = jnp.zeros_like(w1r)
        w1bd = jnp.concatenate(
            [jnp.concatenate([w1r, z], axis=2), jnp.concatenate([z, w1r], axis=2)], axis=1)
        z2 = jnp.zeros_like(w2)
        w2bd = jnp.concatenate(
            [jnp.concatenate([w2, z2], axis=1), jnp.concatenate([z2, w2], axis=1)], axis=0)
        return jnp.concatenate([pe, pe], axis=1), w1bd.astype(BF16), w2bd.astype(BF16)

    k, v = one(pe_k, w1_k, w2_k), one(pe_v, w1_v, w2_v)
    return tuple(jnp.stack([a, b]) for a, b in zip(k, v))


def _compress_pages_kernel(pt_ref, *refs, n_pages):
    del pt_ref
    pages = refs[:n_pages]
    pe_ref, w1_ref, w2_ref, o_ref = refs[n_pages:]
    per_page = 2 * N_KV
    rows = per_page * n_pages
    ridx = lax.broadcasted_iota(jnp.int32, (rows, 1), 0)
    is_v = (_shr(ridx, N_KV) & 1) == 1
    acc = jnp.zeros((rows, 2 * LANES), F32)
    for d in range(HEAD_DIM):
        a = jnp.concatenate([pg[0, pl.ds(d, per_page, stride=HEAD_DIM), :] for pg in pages], axis=0)
        ak = jnp.where(is_v, 0.0, a + pe_ref[0, d:d + 1, :])
        av = jnp.where(is_v, a + pe_ref[1, d:d + 1, :], 0.0)
        lhs = jnp.concatenate([ak, av], axis=1).astype(BF16)
        acc = acc + jnp.dot(lhs, w1_ref[d], preferred_element_type=F32)
    hid = jax.nn.gelu(acc)
    lhs = jnp.concatenate([jnp.where(is_v, 0.0, hid), jnp.where(is_v, hid, 0.0)], axis=1)
    o_ref[...] = jnp.dot(lhs.astype(BF16), w2_ref[...], preferred_element_type=F32)


def _compress_pages(cache_t, page_table, layer, n_phys, pe_t, w1p, w2p):
    nseq, n_pages = page_table.shape
    rows = 2 * N_KV * n_pages

    def page_map(p_i):
        return lambda b, pt: (layer * n_phys + pt[b * n_pages + p_i], 0, 0)

    in_specs = [pl.BlockSpec((1,) + cache_t.shape[1:], page_map(p_i)) for p_i in range(n_pages)]
    in_specs += [
        pl.BlockSpec(pe_t.shape, lambda b, pt: (0, 0, 0)),
        pl.BlockSpec(w1p.shape, lambda b, pt: (0, 0, 0)),
        pl.BlockSpec(w2p.shape, lambda b, pt: (0, 0)),
    ]
    grid_spec = pltpu.PrefetchScalarGridSpec(
        num_scalar_prefetch=1,
        grid=(nseq,),
        in_specs=in_specs,
        out_specs=pl.BlockSpec((rows, LANES), lambda b, pt: (b, 0)),
    )
    return pl.pallas_call(
        functools.partial(_compress_pages_kernel, n_pages=n_pages),
        grid_spec=grid_spec,
        out_shape=jax.ShapeDtypeStruct((nseq * rows, LANES), F32),
        compiler_params=_params("arbitrary"),
    )(page_table.reshape(-1), *([cache_t] * n_pages), pe_t, w1p, w2p)


def _compress_page_weights(pe_k, w1_k, w2_k, pe_v, w1_v, w2_v):
    eye = jnp.eye(2, dtype=F32)
    hid = w1_k.shape[1]

    def w1p(w1):
        w1d = w1.reshape(BLK, HEAD_DIM, hid).transpose(1, 0, 2)
        return jnp.einsum('drj,bc->dbrcj', w1d, eye)

    def w2p(w2):
        return jnp.einsum('je,bc->bjce', w2, eye)

    w1 = jnp.stack([w1p(w1_k), w1p(w1_v)], axis=1).reshape(HEAD_DIM, 4 * BLK, 2 * hid)
    w2 = jnp.stack([w2p(w2_k), w2p(w2_v)]).reshape(4 * hid, 2 * HEAD_DIM)
    pe_t = jnp.stack([jnp.tile(pe_k.T, (1, 2)), jnp.tile(pe_v.T, (1, 2))])
    return pe_t, w1.astype(BF16), w2.astype(BF16)


def _pool_kernel(u_ref, halo_ref, w_ref, s_ref, o_ref, *, tt, tiles_per_seq):
    i = pl.program_id(0)
    u = u_ref[...]
    halo = halo_ref[...]
    if tiles_per_seq is not None:
        halo = jnp.where(i % tiles_per_seq == 0, 0.0, halo)
        pos = (i % tiles_per_seq) * tt + lax.broadcasted_iota(jnp.int32, (tt, 1), 0)
    a = jnp.concatenate([halo, u], axis=0)
    gw = u.shape[1] // len(POOL_WINDOWS)
    for gi, w in enumerate(POOL_WINDOWS):
        b = a[:, gi * gw:(gi + 1) * gw]
        width = 1
        while width < w:
            b = b[:b.shape[0] - width] + b[width:]
            width *= 2
        lo = POOL_HALO + 1 - w
        win = b[lo:lo + tt]
        if tiles_per_seq is None:
            mean = win / float(w)
        else:
            mean = win / jnp.minimum(pos + 1, w).astype(F32)
        d = mean - u[:, gi * gw:(gi + 1) * gw]
        y = jnp.dot(d.astype(BF16), w_ref[gi], preferred_element_type=F32)
        o_ref[:, gi * gw:(gi + 1) * gw] = (y * s_ref[:, gi * gw:(gi + 1) * gw]).astype(o_ref.dtype)


def _pool_mix(u, m, w_grp, scale, tt, tiles_per_seq, col=0):
    width = scale.shape[0]
    hb = tt // POOL_HALO
    return pl.pallas_call(
        functools.partial(_pool_kernel, tt=tt, tiles_per_seq=tiles_per_seq),
        grid=(m // tt,),
        in_specs=[
            pl.BlockSpec((tt, width), lambda i: (i, col)),
            pl.BlockSpec((POOL_HALO, width), lambda i: (jnp.maximum(i * hb - 1, 0), col)),
            pl.BlockSpec(w_grp.shape, lambda i: (0, 0, 0)),
            pl.BlockSpec((1, width), lambda i: (0, 0)),
        ],
        out_specs=pl.BlockSpec((tt, width), lambda i: (i, 0)),
        out_shape=jax.ShapeDtypeStruct((m, width), BF16),
        compiler_params=_params("parallel"),
    )(u, u, w_grp, scale.reshape(1, width))


def _gated_branch_kernel(a_ref, b_ref, wa_ref, wb_ref, x_ref, g_ref, wga_ref, wgb_ref, o_ref, xn_ref):
    @pl.when(pl.program_id(1) == 0)
    def _():
        x = x_ref[...]
        ms = jnp.mean(x * x, axis=-1, keepdims=True)
        xn_ref[...] = (x * lax.rsqrt(ms + EPS) * g_ref[...]).astype(BF16)

    xn = xn_ref[...]
    ga = jax.nn.sigmoid(lax.dot_general(xn, wga_ref[0], _NT, preferred_element_type=F32))
    gb = jax.nn.sigmoid(lax.dot_general(xn, wgb_ref[0], _NT, preferred_element_type=F32))
    a = jnp.dot(a_ref[...], wa_ref[0], preferred_element_type=F32)
    b = jnp.dot(b_ref[...], wb_ref[0], preferred_element_type=F32)
    o_ref[...] = (ga * a + gb * b).astype(o_ref.dtype)


def _gated_branches(a, b, wa, wb, x, g, wg_t, layer, tm=1024, tn=512):
    m, k = a.shape
    d = x.shape[1]
    n = wa.shape[2]
    nj = n // tn
    return pl.pallas_call(
        _gated_branch_kernel,
        grid=(m // tm, nj),
        in_specs=[
            pl.BlockSpec((tm, k), lambda i, j: (i, 0)),
            pl.BlockSpec((tm, k), lambda i, j: (i, 0)),
            pl.BlockSpec((1, k, tn), lambda i, j: (layer, 0, j)),
            pl.BlockSpec((1, k, tn), lambda i, j: (layer, 0, j)),
            pl.BlockSpec((tm, d), lambda i, j: (i, 0)),
            pl.BlockSpec((1, d), lambda i, j: (0, 0)),
            pl.BlockSpec((1, tn, d), lambda i, j: (layer, j, 0)),
            pl.BlockSpec((1, tn, d), lambda i, j: (layer, j + nj, 0)),
        ],
        out_specs=pl.BlockSpec((tm, tn), lambda i, j: (i, j)),
        out_shape=jax.ShapeDtypeStruct((m, n), BF16),
        scratch_shapes=[pltpu.VMEM((tm, d), BF16)],
        compiler_params=_params("parallel", "arbitrary"),
    )(a, b, wa, wb, x, g.reshape(1, d), wg_t, wg_t)


def _resid_matmul_kernel(x_ref, a_ref, w_ref, o_ref):
    o_ref[...] = x_ref[...] + jnp.dot(a_ref[...], w_ref[0], preferred_element_type=F32)


def _resid_matmul(x, a, w, layer, tm=1024, tn=512):
    m, k = a.shape
    n = w.shape[2]
    return pl.pallas_call(
        _resid_matmul_kernel,
        grid=(m // tm, n // tn),
        in_specs=[
            pl.BlockSpec((tm, tn), lambda i, j: (i, j)),
            pl.BlockSpec((tm, k), lambda i, j: (i, 0)),
            pl.BlockSpec((1, k, tn), lambda i, j: (layer, 0, j)),
        ],
        out_specs=pl.BlockSpec((tm, tn), lambda i, j: (i, j)),
        out_shape=jax.ShapeDtypeStruct((m, n), F32),
        compiler_params=_params("parallel", "arbitrary"),
    )(x, a, w)


def _mlp_kernel(x_ref, g_ref, wu_ref, wd_ref, gf_ref, o_ref, h_ref, *, final_norm):
    f = pl.program_id(1)

    @pl.when(f == 0)
    def _():
        x = x_ref[...]
        ms = jnp.mean(x * x, axis=-1, keepdims=True)
        h_ref[...] = (x * lax.rsqrt(ms + EPS) * g_ref[...]).astype(BF16)
        o_ref[...] = x

    up = jnp.dot(h_ref[...], wu_ref[0], preferred_element_type=F32)
    act = jnp.square(jnp.maximum(up, 0.0)).astype(BF16)
    o_ref[...] += jnp.dot(act, wd_ref[0], preferred_element_type=F32)

    if final_norm:
        @pl.when(f == pl.num_programs(1) - 1)
        def _():
            y = o_ref[...]
            ms = jnp.mean(y * y, axis=-1, keepdims=True)
            o_ref[...] = y * lax.rsqrt(ms + EPS) * gf_ref[...]


def _mlp(x, g, w_up, w_down, g_final, final_norm, layer, row0=0, rows=None, tm=1024, tf=512):
    d = x.shape[1]
    m = x.shape[0] if rows is None else rows
    dff = w_up.shape[2]
    assert row0 % tm == 0 and m % tm == 0
    return pl.pallas_call(
        functools.partial(_mlp_kernel, final_norm=final_norm),
        grid=(m // tm, dff // tf),
        in_specs=[
            pl.BlockSpec((tm, d), lambda i, f: (row0 // tm + i, 0)),
            pl.BlockSpec((1, d), lambda i, f: (0, 0)),
            pl.BlockSpec((1, d, tf), lambda i, f: (layer, 0, f)),
            pl.BlockSpec((1, tf, d), lambda i, f: (layer, f, 0)),
            pl.BlockSpec((1, d), lambda i, f: (0, 0)),
        ],
        out_specs=pl.BlockSpec((tm, d), lambda i, f: (i, 0)),
        out_shape=jax.ShapeDtypeStruct((m, d), F32),
        scratch_shapes=[pltpu.VMEM((tm, d), BF16)],
        compiler_params=_params("parallel", "arbitrary"),
    )(x, g.reshape(1, d), w_up, w_down, g_final.reshape(1, d))


def _shr(x, pow2):
    assert pow2 & (pow2 - 1) == 0
    return lax.shift_right_logical(x, pow2.bit_length() - 1)


def _masked_softmax(s, mask):
    sm = jnp.where(mask, s, NEG)
    e = jnp.exp(sm - jnp.max(sm, axis=-1, keepdims=True))
    return e / jnp.sum(e, axis=-1, keepdims=True)


def _select_blocks(imp, qpos, n_loop):
    r, nb = imp.shape
    bidx = lax.broadcasted_iota(jnp.int32, (r, nb), 1)
    cur = _shr(qpos, BLK)
    forced = (bidx == 0) | (bidx == cur) | (bidx == cur - 1)
    cand = bidx * BLK <= qpos
    score = jnp.where(cand, jnp.where(forced, FORCED, imp), NEG)
    rank = jnp.zeros((r, nb), F32)
    for j in range(n_loop):
        c = score[:, j:j + 1]
        beats = (c > score) | ((c == score) & (bidx > j))
        rank = rank + jnp.where(beats, 1.0, 0.0)
    return jnp.where(cand & (rank < TOP_N), 1.0, 0.0)


def _select_blocks_t(imp_t, qpos_t):
    nb, r = imp_t.shape
    bidx = lax.broadcasted_iota(jnp.int32, (nb, r), 0)
    cur = _shr(qpos_t, BLK)
    forced = (bidx == 0) | (bidx == cur) | (bidx == cur - 1)
    cand = bidx * BLK <= qpos_t
    score = jnp.where(cand, jnp.where(forced, FORCED, imp_t), NEG)
    n_chunks = nb // SUBLANES
    chunks = [score[SUBLANES * c:SUBLANES * (c + 1)] for c in range(n_chunks)]
    ranks = [jnp.zeros((SUBLANES, r), F32) for _ in range(n_chunks)]
    sub = lax.broadcasted_iota(jnp.int32, (SUBLANES, r), 0)
    for j in range(nb):
        cj = score[j:j + 1, :]
        for c in range(n_chunks):
            lo = SUBLANES * c
            ge = jnp.where(cj >= chunks[c], 1.0, 0.0)
            gt = jnp.where(cj > chunks[c], 1.0, 0.0)
            if lo > j:
                term = ge
            elif lo + SUBLANES - 1 < j:
                term = gt
            else:
                term = jnp.where(sub > j - lo, ge, gt)
            ranks[c] = ranks[c] + term
    rank = jnp.concatenate(ranks, axis=0)
    return jnp.where(cand & (rank < TOP_N), 1.0, 0.0)


def _block_expand(n_blocks, k0, n_keys):
    b = lax.broadcasted_iota(jnp.int32, (n_blocks, n_keys), 0)
    k = lax.broadcasted_iota(jnp.int32, (n_blocks, n_keys), 1) + k0
    return jnp.where(_shr(k, BLK) == b, 1.0, 0.0).astype(BF16)


def _softmax_keys(scores, mask):
    sm = jnp.where(mask, scores, NEG)
    e = jnp.exp(sm - jnp.max(sm, axis=0, keepdims=True))
    return e / jnp.sum(e, axis=0, keepdims=True)


def _nsa_prompt_kernel(q_ref, kc_ref, vc_ref, ks_ref, vs_ref, kw_ref, vw_ref, g_ref,
                       o_ref, *, seq, heads):
    i = pl.program_id(2)
    rows = GROUP * Q_TILE
    hw = GROUP * HEAD_DIM
    nb = seq // BLK
    qpos = i * Q_TILE + (lax.broadcasted_iota(jnp.int32, (1, rows), 1) & (Q_TILE - 1))
    bcol = lax.broadcasted_iota(jnp.int32, (nb, 1), 0)
    bidx = lax.broadcasted_iota(jnp.int32, (1, nb), 1)
    cmask = bcol * BLK + (BLK - 1) <= qpos
    own = _shr(bidx, Q_TILE // BLK) == i

    def prepare(h):
        qt = q_ref[:, h * hw:(h + 1) * hw].astype(BF16)
        q = jnp.concatenate([qt[:, g * HEAD_DIM:(g + 1) * HEAD_DIM] for g in range(GROUP)], axis=0)
        s_c = lax.dot_general(kc_ref[0, h], q, _NT, preferred_element_type=F32)
        p_c = jnp.where(cmask, _softmax_keys(s_c, cmask), 0.0)
        o_cmp = jnp.dot(vc_ref[0, h], p_c.astype(BF16), preferred_element_type=F32)
        imp = p_c[:, 0:Q_TILE]
        for g in range(1, GROUP):
            imp = imp + p_c[:, g * Q_TILE:(g + 1) * Q_TILE]
        sel = _select_blocks_t(imp, qpos[:, 0:Q_TILE]).T
        block_bias = jnp.where((sel > 0.5) & jnp.logical_not(own), 0.0, NEG).astype(BF16)
        if nb < HEAD_DIM:
            block_bias = jnp.concatenate([block_bias, jnp.zeros((Q_TILE, HEAD_DIM - nb), BF16)], axis=1)
        q_aug = jnp.concatenate([q, jnp.concatenate([block_bias] * GROUP, axis=0)], axis=1)
        q_pad = jnp.concatenate([q, jnp.zeros((rows, HEAD_DIM), BF16)], axis=1)
        return q_pad, q_aug, o_cmp

    def online_step(carry, scores, v_t):
        m, l, acc = carry
        m_new = jnp.maximum(m, jnp.max(scores, axis=0, keepdims=True))
        alpha = jnp.exp(m - m_new)
        pj = jnp.exp(scores - m_new)
        l = alpha * l + jnp.sum(pj, axis=0, keepdims=True)
        acc = alpha * acc + jnp.dot(v_t, pj.astype(BF16), preferred_element_type=F32)
        return m_new, l, acc

    prepared = [prepare(h) for h in range(heads)]

    def body(j, carries):
        k0 = pl.multiple_of(j * KEY_TILE, KEY_TILE)
        scores = [lax.dot_general(ks_ref[pl.ds(k0, KEY_TILE), h * LANES:(h + 1) * LANES],
                                  prepared[h][1], _NT, preferred_element_type=F32)
                  for h in range(heads)]
        return tuple(online_step(carries[h], scores[h],
                                 vs_ref[h * HEAD_DIM:(h + 1) * HEAD_DIM, pl.ds(k0, KEY_TILE)])
                     for h in range(heads))

    n_tiles = lax.div(i * Q_TILE + KEY_TILE - 1, KEY_TILE)
    init = tuple((jnp.full((1, rows), NEG, F32), jnp.zeros((1, rows), F32),
                  jnp.zeros((HEAD_DIM, rows), F32)) for _ in range(heads))
    carries = lax.fori_loop(0, n_tiles, body, init)

    d0 = pl.multiple_of(i * Q_TILE, Q_TILE)
    causal = d0 + lax.broadcasted_iota(jnp.int32, (Q_TILE, 1), 0) <= qpos
    span = WINDOW + Q_TILE
    start = pl.multiple_of(jnp.maximum(i * Q_TILE - WINDOW, 0), Q_TILE)
    wpos = start + lax.broadcasted_iota(jnp.int32, (span, 1), 0)
    wmask = (wpos <= qpos) & (wpos >= qpos - WINDOW)
    sd = [lax.dot_general(ks_ref[pl.ds(d0, Q_TILE), h * LANES:(h + 1) * LANES], prepared[h][0],
                          _NT, preferred_element_type=F32) for h in range(heads)]
    sw = [lax.dot_general(kw_ref[pl.ds(start, span), h * LANES:(h + 1) * LANES], prepared[h][0],
                          _NT, preferred_element_type=F32) for h in range(heads)]
    for h in range(heads):
        o_cmp = prepared[h][2]
        _, l, acc = online_step(carries[h], jnp.where(causal, sd[h], NEG),
                                vs_ref[h * HEAD_DIM:(h + 1) * HEAD_DIM, pl.ds(d0, Q_TILE)])
        o_sel = acc / l
        vw_t = vw_ref[h * HEAD_DIM:(h + 1) * HEAD_DIM, pl.ds(start, span)]
        o_win = jnp.dot(vw_t, _softmax_keys(sw[h], wmask).astype(BF16), preferred_element_type=F32)

        g_t = g_ref[:, h * LANES:(h + 1) * LANES].T
        for g in range(GROUP):
            sl = slice(g * Q_TILE, (g + 1) * Q_TILE)
            o_g = (g_t[g:g + 1] * o_cmp[:, sl] + g_t[GROUP + g:GROUP + g + 1] * o_sel[:, sl]
                   + g_t[2 * GROUP + g:2 * GROUP + g + 1] * o_win[:, sl])
            c0 = h * hw + g * HEAD_DIM
            o_ref[:, c0:c0 + HEAD_DIM] = o_g.T.astype(o_ref.dtype)


def _nsa_prompt(q, q_col, kc, vc, keys, vals_t, gates, gate_col, n, seq, heads=4):
    nq = seq // Q_TILE
    nb = seq // BLK
    assert nb <= HEAD_DIM
    row = lambda b, h, i: (b * nq + i, h)
    head = lambda b, h, i: (b, h, 0, 0)
    steps = N_KV // heads
    return pl.pallas_call(
        functools.partial(_nsa_prompt_kernel, seq=seq, heads=heads),
        grid=(n, steps, nq),
        in_specs=[
            pl.BlockSpec((Q_TILE, heads * GROUP * HEAD_DIM), lambda b, h, i: (b * nq + i, q_col * steps + h)),
            pl.BlockSpec((1, heads, nb, HEAD_DIM), head), pl.BlockSpec((1, heads, HEAD_DIM, nb), head),
            pl.BlockSpec((seq, heads * LANES), lambda b, h, i: (b, h)),
            pl.BlockSpec((heads * HEAD_DIM, seq), lambda b, h, i: (h, b)),
            pl.BlockSpec((seq, heads * LANES), lambda b, h, i: (b, steps + h)),
            pl.BlockSpec((heads * HEAD_DIM, seq), lambda b, h, i: (steps + h, b)),
            pl.BlockSpec((Q_TILE, heads * LANES), lambda b, h, i: (b * nq + i, gate_col * steps + h)),
        ],
        out_specs=pl.BlockSpec((Q_TILE, heads * GROUP * HEAD_DIM), row),
        out_shape=jax.ShapeDtypeStruct((n * seq, N_KV * GROUP * HEAD_DIM), BF16),
        compiler_params=_params("parallel", "parallel", "arbitrary"),
    )(q, kc, vc, keys, vals_t, keys, vals_t, gates)


def _nsa_sample_kernel(pt_ref, q_ref, g_ref, *refs, n_pages, page, dec, wb, nsq):
    del pt_ref
    cmp_seq, cmp_new = refs[:2]
    sel_pages = refs[2:nsq * n_pages + 2]
    sel_new = refs[nsq * n_pages + 2]
    win_ref, win_new, _, o_ref, wo_ref, kc_ref, k_ref, v_ref, kw_ref, vw_ref = refs[nsq * n_pages + 3:]
    seqs = range(nsq)
    past = n_pages * page
    rows = N_KV * GROUP * dec
    ridx = lax.broadcasted_iota(jnp.int32, (rows, 1), 0)
    qpos = past + (ridx & (dec - 1))
    q = [q_ref[s] for s in seqs]

    n_keys = past + page
    for s in seqs:
        for p_i in range(n_pages + 1):
            pg = sel_pages[s * n_pages + p_i][0] if p_i < n_pages else sel_new[s]
            k_ref[s, :, p_i * page:(p_i + 1) * page] = pg[:KV_W].astype(BF16)
            v_ref[s, :, p_i * page:(p_i + 1) * page] = pg[KV_W:].astype(BF16)
    wn = [win_ref[s] for s in seqs]
    nw = [win_new[s] for s in seqs]
    for s in seqs:
        kw_ref[s, :, 0:wb] = wn[s][:KV_W].astype(BF16)
        vw_ref[s, :, 0:wb] = wn[s][KV_W:].astype(BF16)
        kw_ref[s, :, wb:wb + page] = nw[s][:KV_W].astype(BF16)
        vw_ref[s, :, wb:wb + page] = nw[s][KV_W:].astype(BF16)

    bpp = page // BLK
    nbp = kc_ref.shape[1]
    for s in seqs:
        kc_ref[s] = jnp.zeros(kc_ref.shape[1:], F32)
        kc_ref[s, 0:bpp * n_pages, :] = cmp_seq[s]
        kc_ref[s, bpp * n_pages:bpp * n_pages + 1, :] = cmp_new[s, 0:1, :]
    sc = [lax.dot_general(q[s], kc_ref[s, :, :KV_W].astype(BF16), _NT, preferred_element_type=F32)
          for s in seqs]
    bidx = lax.broadcasted_iota(jnp.int32, (1, nbp), 1)
    cmask = bidx * BLK + (BLK - 1) <= qpos
    p = [jnp.where(cmask, _masked_softmax(sc[s], cmask), 0.0) for s in seqs]
    o_cmp = [jnp.dot(p[s].astype(BF16), kc_ref[s, :, KV_W:].astype(BF16), preferred_element_type=F32)
             for s in seqs]

    qpos_ht = past + (lax.broadcasted_iota(jnp.int32, (N_KV * dec, 1), 0) & (dec - 1))
    sel_rows = []
    for s in seqs:
        imps = []
        for h in range(N_KV):
            base = h * GROUP * dec
            acc = p[s][base:base + dec]
            for g in range(1, GROUP):
                acc = acc + p[s][base + g * dec:base + (g + 1) * dec]
            imps.append(acc)
        imp = jnp.concatenate(imps, axis=0)
        sel = _select_blocks(imp, qpos_ht, bpp * n_pages + 1)
        sel_rows.append(jnp.concatenate(
            [sel[h * dec:(h + 1) * dec] for h in range(N_KV) for _ in range(GROUP)], axis=0))

    sw = [jnp.dot(q[s], kw_ref[s], preferred_element_type=F32) for s in seqs]
    wpos = past - wb + lax.broadcasted_iota(jnp.int32, (1, wb + page), 1)
    wmask = (wpos <= qpos) & (wpos >= qpos - WINDOW)
    pw = [_masked_softmax(sw[s], wmask) for s in seqs]
    o_win = [lax.dot_general(pw[s].astype(BF16), vw_ref[s], _NT, preferred_element_type=F32)
             for s in seqs]

    ss = [jnp.dot(q[s], k_ref[s], preferred_element_type=F32) for s in seqs]
    expand = _block_expand(nbp, 0, n_keys)
    key_sel = [jnp.dot(sel_rows[s].astype(BF16), expand, preferred_element_type=F32) for s in seqs]
    kpos = lax.broadcasted_iota(jnp.int32, (1, n_keys), 1)
    ps = [_masked_softmax(ss[s], (key_sel[s] > 0.5) & (kpos <= qpos)) for s in seqs]
    o_sel = [lax.dot_general(ps[s].astype(BF16), v_ref[s], _NT, preferred_element_type=F32)
             for s in seqs]

    col = lax.broadcasted_iota(jnp.int32, (1, KV_W), 1)
    own_head = _shr(ridx, GROUP * dec) == _shr(col, HEAD_DIM)
    lane = lax.broadcasted_iota(jnp.int32, (1, page), 1)
    hr = GROUP * dec
    for s in seqs:
        gt = g_ref[s]
        o = gt[:, 0:1] * o_cmp[s] + gt[:, 1:2] * o_sel[s] + gt[:, 2:3] * o_win[s]
        o = jnp.where(own_head, o, 0.0)
        out = o[0:hr]
        for h in range(1, N_KV):
            out = out + o[h * hr:(h + 1) * hr]
        o_ref[s] = out

        shifted = pltpu.roll(wn[s], wb - dec, 1)
        tail = pltpu.roll(nw[s], page - dec, 1)
        wo_ref[s, :, 0:wb - page] = shifted[:, 0:wb - page]
        wo_ref[s, :, wb - page:wb] = jnp.where(lane >= page - dec, tail, shifted[:, wb - page:wb])


def _nsa_sample(page_table, layer, n_phys, qbd, gates, cmp_seq, cmp_new, sel_cache_t, sel_new,
                win_cache_t, win_new, win_out, nsq=2):
    nseq, n_pages = page_table.shape
    page = sel_cache_t.shape[2]
    rows = qbd.shape[1]
    dec = rows // (N_KV * GROUP)
    wb = win_cache_t.shape[2]
    bpp = page // BLK
    nbp = LANES
    assert bpp * n_pages + 1 <= nbp and wb == WINDOW and page == LANES and dec <= SUBLANES
    assert nseq % nsq == 0
    steps = nseq // nsq

    def page_map(s, p_i):
        return lambda b, pt: (layer * n_phys + pt[(b * nsq + s) * n_pages + p_i], 0, 0)

    seq_map = lambda b, pt: (b, 0, 0)
    layer_seq_map = lambda b, pt: (layer * steps + b, 0, 0)
    in_specs = [
        pl.BlockSpec((nsq, rows, KV_W), seq_map),
        pl.BlockSpec((nsq, rows, 8), seq_map),
        pl.BlockSpec((nsq, bpp * n_pages, 2 * KV_W), seq_map),
        pl.BlockSpec((nsq, 8, 2 * KV_W), seq_map),
    ]
    in_specs += [pl.BlockSpec((1, 2 * KV_W, page), page_map(s, p_i))
                 for s in range(nsq) for p_i in range(n_pages)]
    in_specs += [
        pl.BlockSpec((nsq, 2 * KV_W, page), seq_map),
        pl.BlockSpec((nsq, 2 * KV_W, wb), layer_seq_map),
        pl.BlockSpec((nsq, 2 * KV_W, page), seq_map),
        pl.BlockSpec(memory_space=pl.ANY),
    ]
    n_keys = (n_pages + 1) * page
    grid_spec = pltpu.PrefetchScalarGridSpec(
        num_scalar_prefetch=1,
        grid=(steps,),
        in_specs=in_specs,
        out_specs=[
            pl.BlockSpec((nsq, GROUP * dec, KV_W), seq_map),
            pl.BlockSpec((nsq, 2 * KV_W, wb), layer_seq_map),
        ],
        scratch_shapes=[
            pltpu.VMEM((nsq, nbp, 2 * KV_W), F32),
            pltpu.VMEM((nsq, KV_W, n_keys), BF16),
            pltpu.VMEM((nsq, KV_W, n_keys), BF16),
            pltpu.VMEM((nsq, KV_W, wb + page), BF16),
            pltpu.VMEM((nsq, KV_W, wb + page), BF16),
        ],
    )
    args = (page_table.reshape(-1), qbd, gates, cmp_seq, cmp_new,
            *([sel_cache_t] * (nsq * n_pages)), sel_new, win_cache_t, win_new, win_out)
    return pl.pallas_call(
        functools.partial(_nsa_sample_kernel, n_pages=n_pages, page=page, dec=dec, wb=wb, nsq=nsq),
        grid_spec=grid_spec,
        out_shape=[
            jax.ShapeDtypeStruct((nseq, GROUP * dec, KV_W), F32),
            jax.ShapeDtypeStruct(win_out.shape, F32),
        ],
        input_output_aliases={len(args) - 1: 1},
        compiler_params=_params("arbitrary"),
    )(*args)


def _heads_major(a, n, seq):
    return a.reshape(n, seq, N_KV, HEAD_DIM).transpose(0, 2, 1, 3).astype(BF16)


def _heads_major_t(a, n, seq):
    return a.reshape(n, seq, N_KV, HEAD_DIM).transpose(0, 2, 3, 1).astype(BF16)


def _pad_rows(a, rows):
    return jnp.pad(a, ((0, 0), (0, rows - a.shape[1]), (0, 0)))


def _channel_major(cache):
    depth, a, rows = cache.shape[:3]
    return cache.transpose(0, 1, 3, 4, 5, 2).reshape(depth * a, 2 * KV_W, rows)


def kernel(x_prompt, x_sample, cache_cmp_kv, cache_sel_kv, cache_win_kv, state_pool, page_table, g_mix, w_in, pe_ck, w_ck1, w_ck2, pe_cv, w_cv1, w_cv2, w_pool_grp, pool_scale, w_br_nsa, w_br_pool, w_o, g_mlp, w_up, w_down, g_final):
    n, seq, d = x_prompt.shape
    nseq, dec, _ = x_sample.shape
    depth = w_in.shape[0]
    n_phys, page = cache_cmp_kv.shape[1], cache_cmp_kv.shape[2]
    wb = cache_win_kv.shape[2]
    mp, ms = n * seq, nseq * dec
    nsa_w = N_KV * GROUP * HEAD_DIM
    kv_cols = 2 * KV_W
    n_gate = 3 * N_KV * GROUP
    pool_w = state_pool.shape[-1]
    pool_state = state_pool.shape[2]
    c_q, c_kv, c_ng = nsa_w, nsa_w + 3 * kv_cols, nsa_w + 3 * kv_cols + n_gate
    c_u = c_ng + pool_w

    x = jnp.concatenate([x_prompt.reshape(mp, d), x_sample.reshape(ms, d)], axis=0)
    kv_shape = (2, N_KV, HEAD_DIM)
    outs = [[] for _ in range(8)]
    cmp_cache_t = _channel_major(cache_cmp_kv)
    sel_cache_t = _channel_major(cache_sel_kv)
    win_cache_t = _channel_major(cache_win_kv)
    win_out = jnp.zeros(win_cache_t.shape, F32)

    w_t = w_in.transpose(0, 2, 1)
    w_q = (w_t[:, :c_q] * HEAD_DIM ** -0.5).astype(BF16)
    w_kv = w_t[:, c_q:c_kv].astype(BF16)
    w_ng = w_t[:, c_kv:c_ng].reshape(depth, 3, N_KV, GROUP, d).transpose(0, 2, 1, 3, 4)
    w_ng = jnp.pad(w_ng.reshape(depth, N_KV, 3 * GROUP, d), ((0, 0), (0, 0), (0, LANES - 3 * GROUP), (0, 0)))
    w_ng = w_ng.reshape(depth, N_KV * LANES, d).astype(BF16)
    w_u = w_t[:, c_ng:c_u].astype(BF16)
    w_mg = w_t[:, c_u:].astype(BF16)
    w_cat = jnp.concatenate([w_q, w_u, w_kv, w_ng], axis=1)
    o_u, o_kv, o_ng = nsa_w, nsa_w + pool_w, nsa_w + pool_w + 3 * kv_cols
    assert nsa_w == N_KV * GROUP * HEAD_DIM and o_u % pool_w == 0 and o_ng % (N_KV * LANES) == 0
    def key_slots(rows_t):
        r = rows_t.reshape(depth, N_KV, HEAD_DIM, d)
        return jnp.pad(r, ((0, 0), (0, 0), (0, LANES - HEAD_DIM), (0, 0))).reshape(depth, N_KV * LANES, d)

    c_sel, c_win = c_q + kv_cols, c_q + 2 * kv_cols
    w_keys = jnp.concatenate([key_slots(w_t[:, c_sel:c_sel + KV_W]),
                              key_slots(w_t[:, c_win:c_win + KV_W])], axis=1).astype(BF16)
    w_vals = jnp.concatenate([w_t[:, c_sel + KV_W:c_sel + kv_cols],
                              w_t[:, c_win + KV_W:c_win + kv_cols]], axis=1).astype(BF16)
    w_bn, w_bp, w_ob = w_br_nsa.astype(BF16), w_br_pool.astype(BF16), w_o.astype(BF16)
    w_upb, w_downb = w_up.astype(BF16), w_down.astype(BF16)

    for l in range(depth):
        z = _norm_matmul(x, g_mix[l], w_cat, l, F32, sigmoid_tiles=1)
        zkv = z[:, o_kv:o_ng]
        q = z[:, :nsa_w]
        ng = z[:, o_ng:]

        cmp_w = (pe_ck[l], w_ck1[l], w_ck2[l], pe_cv[l], w_cv1[l], w_cv2[l])
        cw = _compress_weights(*cmp_w)

        cmp_p = _compress(z, mp // BLK, *cw, col0=o_kv)
        kc_p = _heads_major(cmp_p[:, :KV_W], n, seq // BLK)
        vc_p = _heads_major_t(cmp_p[:, KV_W:], n, seq // BLK)
        zp = zkv[:mp]
        keys_p = _norm_matmul(x, g_mix[l], w_keys, l, BF16, onehot_seq=seq, rows=mp)
        vals_p = _norm_matmul_t(x, g_mix[l], w_vals, l, BF16, mp)
        o_nsa_p = _nsa_prompt(z, 0, kc_p, vc_p, keys_p, vals_p, z, o_ng // (N_KV * LANES), n, seq)

        zs = zkv[mp:].reshape(nseq, dec, 3 * kv_cols)
        new_c = _pad_rows(zs[:, :, :kv_cols], BLK)
        lanes_t = lambda a: jnp.pad(a.transpose(0, 2, 1), ((0, 0), (0, 0), (0, page - dec)))
        new_s = lanes_t(zs[:, :, kv_cols:2 * kv_cols])
        new_w = lanes_t(zs[:, :, 2 * kv_cols:])
        bpp = page // BLK
        cmp_seq = _compress_pages(cmp_cache_t, page_table, l, n_phys, *_compress_page_weights(*cmp_w))
        n_pages = page_table.shape[1]
        cmp_seq = cmp_seq.reshape(nseq, n_pages, 2, N_KV, bpp, HEAD_DIM).transpose(0, 1, 4, 2, 3, 5)
        cmp_seq = cmp_seq.reshape(nseq, n_pages * bpp, kv_cols)
        cmp_new = _compress(new_c.reshape(nseq * BLK, kv_cols), nseq, *cw)
        cmp_new = _pad_rows(cmp_new.reshape(nseq, 1, kv_cols), 8)

        qs = q[mp:].astype(BF16).reshape(nseq, dec, N_KV, GROUP, HEAD_DIM)
        eye = jnp.eye(N_KV, dtype=BF16)
        qbd = jnp.einsum('bthgd,hk->bhgtkd', qs, eye).reshape(nseq, N_KV * GROUP * dec, KV_W)
        gs = ng[mp:].reshape(nseq, dec, N_KV, LANES)[..., :3 * GROUP].reshape(nseq, dec, N_KV, 3, GROUP)
        gs = gs.transpose(0, 2, 4, 1, 3).reshape(nseq, N_KV * GROUP * dec, 3)
        gs = jnp.pad(gs, ((0, 0), (0, 0), (0, 5)))
        o_s, win_out = _nsa_sample(page_table, l, n_phys, qbd, gs, cmp_seq, cmp_new, sel_cache_t, new_s,
                                   win_cache_t, new_w, win_out)
        o_nsa_s = o_s.reshape(nseq, GROUP, dec, N_KV, HEAD_DIM).transpose(0, 2, 3, 1, 4)
        o_nsa = jnp.concatenate([o_nsa_p, o_nsa_s.reshape(ms, nsa_w).astype(BF16)], axis=0)

        w_grp = w_pool_grp[l].astype(BF16)
        u = z[:, o_u:o_kv]
        u_p, u_s = u[:mp], u[mp:].reshape(nseq, dec, pool_w)
        o_pool_p = _pool_mix(z, mp, w_grp, pool_scale[l], 512, seq // 512, col=o_u // pool_w)
        hist = jnp.pad(state_pool[l], ((0, 0), (POOL_HALO - pool_state, 0), (0, 0)))
        xx_s = jnp.concatenate([hist, u_s], axis=1)
        rows_s = POOL_HALO + dec
        o_pool_s = _pool_mix(xx_s.reshape(nseq * rows_s, pool_w), nseq * rows_s, w_grp, pool_scale[l],
                             16 * rows_s, None)
        o_pool_s = o_pool_s.reshape(nseq, rows_s, pool_w)[:, POOL_HALO:].reshape(ms, pool_w)
        o_pool = jnp.concatenate([o_pool_p, o_pool_s], axis=0)

        m = _gated_branches(o_nsa, o_pool, w_bn, w_bp, x, g_mix[l], w_mg, l)
        x1 = _resid_matmul(x, m, w_ob, l)
        if l < depth - 1:
            x = _mlp(x1, g_mlp[l], w_upb, w_downb, g_final, False, l)
        else:
            y_prompt = _mlp(x1, g_mlp[l], w_upb, w_downb, g_final, True, l, 0, mp)
            y_sample = _mlp(x1, g_mlp[l], w_upb, w_downb, g_final, True, l, mp, ms)

        zp6 = zp.reshape(n, seq, 3, *kv_shape)
        outs[0].append(zp6[:, :, 0])
        outs[1].append(zp6[:, :, 1])
        outs[2].append(zp6[:, -min(WINDOW, seq):, 2])
        outs[3].append(u_p.reshape(n, seq, pool_w)[:, -pool_state:])
        zs6 = zs.reshape(nseq, dec, 3, *kv_shape)
        outs[4].append(zs6[:, :, 0])
        outs[5].append(zs6[:, :, 1])
        outs[7].append(xx_s[:, -pool_state:])

    y_prompt = y_prompt.reshape(n, seq, d)
    y_sample = y_sample.reshape(nseq, dec, d)
    s_win = win_out.reshape(depth, nseq, *kv_shape, wb).transpose(0, 1, 5, 2, 3, 4)
    stacked = [s_win if i == 6 else jnp.stack(o) for i, o in enumerate(outs)]
    return (y_prompt, y_sample) + tuple(stacked)
```
